```python
import math
import jax, jax.numpy as jnp
from jax import lax
import numpy as np

D_MODEL = 1024
BATCH = 4
SEQ = 8192
DEPTH = 2

CTX_LEN = 256
GRID_W = 64
N_EVEN = (DEPTH + 1) // 2
N_ODD = DEPTH // 2
NORM_EPS = 1e-6
ROPE_THETA = 10000.0
ROPE_DIM = 64
NEG_INF = -1e30

DA_HEADS = 4
DA_QK_DIM = 64
DA_V_DIM = 2 * DA_QK_DIM
DA_QK_W = DA_HEADS * 2 * DA_QK_DIM
WA_Q_HEADS = 8
WA_KV_HEADS = 2
WA_GROUP = WA_Q_HEADS // WA_KV_HEADS
WA_HEAD_DIM = 64
WINDOW = 128
BLOCK = 128

EVEN_WIDTHS = (DA_QK_W, DA_QK_W, DA_HEADS * DA_V_DIM, WA_Q_HEADS * WA_HEAD_DIM,
               WA_KV_HEADS * WA_HEAD_DIM, WA_KV_HEADS * WA_HEAD_DIM)
EVEN_IN = sum(EVEN_WIDTHS)
EVEN_SPLITS = tuple(int(v) for v in np.cumsum(EVEN_WIDTHS)[:-1])
EVEN_MIX = DA_HEADS * DA_V_DIM + WA_Q_HEADS * WA_HEAD_DIM

GLA_HEADS = 4
GLA_K_DIM = 64
GLA_V_DIM = 128
GLA_GATE_RANK = 16
GLA_GATE_NORM = 16.0
GLA_CHUNK = 64
LRU_WIDTH = 512
LRU_BLOCKS = 8
LRU_CONV = 4
CONV_LEFT = (LRU_CONV - 1) // 2
LRU_C = 8.0

ODD_WIDTHS = (GLA_HEADS * GLA_K_DIM, GLA_HEADS * GLA_K_DIM, GLA_HEADS * GLA_V_DIM,
              GLA_HEADS * GLA_V_DIM, 2 * GLA_GATE_RANK, LRU_WIDTH, LRU_WIDTH)
ODD_IN = sum(ODD_WIDTHS)
ODD_SPLITS = tuple(int(v) for v in np.cumsum(ODD_WIDTHS)[:-1])
ODD_MIX = GLA_HEADS * GLA_V_DIM + LRU_WIDTH

D_FF = ((8 * D_MODEL + 3 * 256 - 1) // (3 * 256)) * 256

kernel_name = 'hybrid_diffusion_trunk'


def rms_norm(x, g):
    xf = x.astype(jnp.float32)
    y = xf * lax.rsqrt(jnp.mean(xf * xf, axis=-1, keepdims=True) + NORM_EPS)
    return (y * g.astype(jnp.float32)).astype(x.dtype)


def head_rms(x):
    xf = x.astype(jnp.float32)
    return xf * lax.rsqrt(jnp.mean(xf * xf, axis=-1, keepdims=True) + NORM_EPS)


def modulate(h, shift, scale):
    return h * (1 + scale) + shift


def swiglu(h, w_in, w_out):
    gate, up = jnp.split(h @ w_in, 2, axis=-1)
    return (jax.nn.silu(gate) * up) @ w_out


def rope_tables(n_tokens):
    n_rows = n_tokens // GRID_W
    row = jnp.broadcast_to(jnp.arange(n_rows)[:, None], (n_rows, GRID_W)).reshape(-1)
    col = jnp.broadcast_to(jnp.arange(GRID_W)[None, :], (n_rows, GRID_W)).reshape(-1)
    axis_dim = ROPE_DIM // 2
    inv_freq = ROPE_THETA ** (-jnp.arange(0, axis_dim, 2, dtype=jnp.float32) / axis_dim)
    ang_r = row.astype(jnp.float32)[:, None] * inv_freq
    ang_c = col.astype(jnp.float32)[:, None] * inv_freq
    return (jnp.cos(ang_r), jnp.sin(ang_r), jnp.cos(ang_c), jnp.sin(ang_c))


def _rot_half(x, cos, sin):
    x1, x2 = jnp.split(x, 2, axis=-1)
    return jnp.concatenate([x1 * cos - x2 * sin, x1 * sin + x2 * cos], axis=-1)


def apply_axial_rope(x, tables):
    shape = (1, x.shape[1]) + (1,) * (x.ndim - 3) + (-1,)
    cos_r, sin_r, cos_c, sin_c = (t.reshape(shape) for t in tables)
    xf = x.astype(jnp.float32)
    half = x.shape[-1] // 2
    out = jnp.concatenate([_rot_half(xf[..., :half], cos_r, sin_r),
                           _rot_half(xf[..., half:], cos_c, sin_c)], axis=-1)
    return out.astype(x.dtype)


def diff_attn_core(q, k, v, lam):
    s = jnp.einsum('bqhmd,bkhmd->bhmqk', q, k).astype(jnp.float32) * (DA_QK_DIM ** -0.5)
    p = jax.nn.softmax(s, axis=-1)
    w = p[:, :, 0] - lam * p[:, :, 1]
    return jnp.einsum('bhqk,bkhd->bqhd', w.astype(v.dtype), v)


def diff_head_out(o, lam_init):
    y = head_rms(o) * (1.0 - lam_init)
    return y.reshape(o.shape[0], o.shape[1], -1).astype(o.dtype)


def gqa_scores(q, k):
    return jnp.einsum('bqhgd,bkhd->bhgqk', q, k).astype(jnp.float32) * (WA_HEAD_DIM ** -0.5)


def gqa_values(p, v):
    return jnp.einsum('bhgqk,bkhd->bqhgd', p.astype(v.dtype), v)


def sink_softmax(scores, sink):
    s_sink = jnp.broadcast_to(sink[None, :, :, None, None], scores[0].shape[:-1] + (1,))
    p = jax.nn.softmax(jnp.concatenate([s_sink] + scores, axis=-1), axis=-1)
    bounds = [1]
    for s in scores:
        bounds.append(bounds[-1] + s.shape[-1])
    return [p[..., bounds[i]:bounds[i + 1]] for i in range(len(scores))]


def window_gqa(q, k, v, kc, vc, sink):
    B, L = q.shape[0], q.shape[1]
    nb = L // BLOCK

    def band(t):
        tp = jnp.pad(t, ((0, 0), (BLOCK, BLOCK)) + ((0, 0),) * (t.ndim - 2))
        tp = tp.reshape((B, nb + 2, BLOCK) + t.shape[2:])
        win = jnp.concatenate([tp[:, :-2], tp[:, 1:-1], tp[:, 2:]], axis=2)
        return win.swapaxes(0, 1)

    q_blocks = q.reshape((B, nb, BLOCK) + q.shape[2:]).swapaxes(0, 1)
    offs_q = jnp.arange(BLOCK)
    offs_k = jnp.arange(3 * BLOCK) - BLOCK
    rel_ok = jnp.abs(offs_k[None, :] - offs_q[:, None]) <= WINDOW

    def one_block(args):
        qq, kk, vv, bi = args
        kpos = bi * BLOCK + offs_k
        valid = rel_ok & ((kpos >= 0) & (kpos < L))[None, :]
        s_win = jnp.where(valid, gqa_scores(qq, kk), NEG_INF)
        p_ctx, p_win = sink_softmax([gqa_scores(qq, kc), s_win], sink)
        return gqa_values(p_ctx, vc) + gqa_values(p_win, vv)

    out = lax.map(one_block, (q_blocks, band(k), band(v), jnp.arange(nb)))
    return out.swapaxes(0, 1).reshape(B, L, -1)


def ctx_gqa(qc, kc, vc, sink):
    (p,) = sink_softmax([gqa_scores(qc, kc)], sink)
    o = gqa_values(p, vc)
    return o.reshape(o.shape[0], o.shape[1], -1)


def _split_even(p):
    B, T = p.shape[0], p.shape[1]
    qa, ka, va, qb, kb, vb = jnp.split(p, EVEN_SPLITS, axis=-1)
    return (qa.reshape(B, T, DA_HEADS, 2, DA_QK_DIM), ka.reshape(B, T, DA_HEADS, 2, DA_QK_DIM),
            va.reshape(B, T, DA_HEADS, DA_V_DIM),
            qb.reshape(B, T, WA_KV_HEADS, WA_GROUP, WA_HEAD_DIM),
            kb.reshape(B, T, WA_KV_HEADS, WA_HEAD_DIM), vb.reshape(B, T, WA_KV_HEADS, WA_HEAD_DIM))


def even_mixer(hc, hl, w_in, w_out, lam_vec, sink, lam_init, rope, need_ctx):
    B, L = hl.shape[0], hl.shape[1]
    qa_c, ka_c, va_c, qb_c, kb_c, vb_c = _split_even(hc @ w_in)
    qa_l, ka_l, va_l, qb_l, kb_l, vb_l = _split_even(hl @ w_in)
    qa_l, ka_l, qb_l, kb_l = (apply_axial_rope(t, rope) for t in (qa_l, ka_l, qb_l, kb_l))
    lv = lam_vec.astype(jnp.float32)
    lam = jnp.exp(jnp.sum(lv[0] * lv[1])) - jnp.exp(jnp.sum(lv[2] * lv[3])) + lam_init
    sink_g = sink.astype(jnp.float32).reshape(WA_KV_HEADS, WA_GROUP)

    k_all = jnp.concatenate([ka_c, ka_l], axis=1)
    v_all = jnp.concatenate([va_c, va_l], axis=1)
    nb = L // BLOCK
    qa_blocks = qa_l.reshape((B, nb, BLOCK) + qa_l.shape[2:]).swapaxes(0, 1)
    a_l = lax.map(lambda qq: diff_attn_core(qq, k_all, v_all, lam), qa_blocks)
    a_l = diff_head_out(a_l.swapaxes(0, 1).reshape(B, L, DA_HEADS, DA_V_DIM), lam_init)
    b_l = window_gqa(qb_l, kb_l, vb_l, kb_c, vb_c, sink_g)
    y_l = jnp.concatenate([a_l, b_l], axis=-1) @ w_out
    y_c = None
    if need_ctx:
        a_c = diff_head_out(diff_attn_core(qa_c, ka_c, va_c, lam), lam_init)
        b_c = ctx_gqa(qb_c, kb_c, vb_c, sink_g)
        y_c = jnp.concatenate([a_c, b_c], axis=-1) @ w_out
    return y_c, y_l


def gla_chunked(q, k, v, log_a, S0):
    B, T, H = q.shape[0], q.shape[1], q.shape[2]
    C = GLA_CHUNK
    n = T // C

    def to_chunks(t):
        return t.reshape(B, n, C, H, t.shape[-1]).transpose(1, 0, 3, 2, 4)

    causal = jnp.tril(jnp.ones((C, C), dtype=bool))[:, :, None]

    def step(S, inp):
        qc, kc, vc, gc = inp
        b = jnp.cumsum(gc, axis=2)
        b_last = b[:, :, -1]
        o_inter = jnp.einsum('bhcd,bhde->bhce', qc * jnp.exp(b), S)
        diff = jnp.where(causal, b[:, :, :, None, :] - b[:, :, None, :, :], -jnp.inf)
        A = jnp.einsum('bhid,bhjd,bhijd->bhij', qc, kc, jnp.exp(diff))
        o = o_inter + jnp.einsum('bhij,bhje->bhie', A, vc)
        S_new = jnp.exp(b_last)[..., None] * S + jnp.einsum(
            'bhcd,bhce->bhde', kc * jnp.exp(b_last[:, :, None] - b), vc)
        return S_new, o

    S, o = lax.scan(step, S0, (to_chunks(q), to_chunks(k), to_chunks(v), to_chunks(log_a)))
    o = o.transpose(1, 0, 3, 2, 4).reshape(B, T, H, v.shape[-1])
    return o, S


def centred_conv(x, w, b):
    T = x.shape[1]
    xp = jnp.pad(x, ((0, 0), (CONV_LEFT, LRU_CONV - 1 - CONV_LEFT), (0, 0)))
    return sum(xp[:, j:j + T] * w[j] for j in range(LRU_CONV)) + b


def rglru_coeffs(x, wa, ba, wx, bx, lam):
    f32 = jnp.float32
    xb = x.reshape(x.shape[:-1] + (LRU_BLOCKS, LRU_WIDTH // LRU_BLOCKS))
    r = jax.nn.sigmoid(jnp.einsum('btnc,ncd->btnd', xb, wa.astype(f32)).reshape(x.shape) + ba.astype(f32))
    i = jax.nn.sigmoid(jnp.einsum('btnc,ncd->btnd', xb, wx.astype(f32)).reshape(x.shape) + bx.astype(f32))
    log_a = -LRU_C * r * jax.nn.softplus(-lam.astype(f32))
    a = jnp.exp(log_a)
    u = jnp.sqrt(-jnp.expm1(2.0 * log_a)) * (i * x)
    return a, u


def linear_scan(a, u, h0):
    u = u.at[:, 0].add(a[:, 0] * h0)

    def combine(lhs, rhs):
        return lhs[0] * rhs[0], rhs[0] * lhs[1] + rhs[1]

    _, h = lax.associative_scan(combine, (a, u), axis=1)
    return h


def odd_mixer(hc, hl, w_in, w_out, gate_w, gate_b, gla_g, conv_w, conv_b,
              wa, ba, wx, bx, lam, need_ctx):
    f32 = jnp.float32
    B = hl.shape[0]

    def project(h):
        T = h.shape[1]
        q, k, v, g, lr, zg, zx = jnp.split(h @ w_in, ODD_SPLITS, axis=-1)
        q = q.astype(f32).reshape(B, T, GLA_HEADS, GLA_K_DIM) * (GLA_K_DIM ** -0.5)
        k = k.astype(f32).reshape(B, T, GLA_HEADS, GLA_K_DIM)
        v = v.astype(f32).reshape(B, T, GLA_HEADS, GLA_V_DIM)
        lr = lr.astype(f32).reshape(B, T, 2, GLA_GATE_RANK)
        logit = jnp.einsum('btnr,nrk->btnk', lr, gate_w.astype(f32)) + gate_b.astype(f32)
        log_a = (jax.nn.log_sigmoid(logit) / GLA_GATE_NORM).reshape(B, T, 2, GLA_HEADS, GLA_K_DIM)
        xr = centred_conv(zx.astype(f32), conv_w.astype(f32), conv_b.astype(f32))
        return q, k, v, log_a, g, zg, xr

    qc, kc, vc, lac, gc, zgc, xrc = project(hc)
    ql, kl, vl, lal, gl, zgl, xrl = project(hl)
    S0 = jnp.zeros((B, GLA_HEADS, GLA_K_DIM, GLA_V_DIM), f32)
    h0 = jnp.zeros((B, LRU_WIDTH), f32)
    gla_c, gla_l, lru_c, lru_l = [], [], [], []
    for d in range(2):
        f = (lambda t: jnp.flip(t, axis=1)) if d == 1 else (lambda t: t)
        oc, S_ctx = gla_chunked(f(qc), f(kc), f(vc), f(lac[:, :, d]), S0)
        ol, _ = gla_chunked(f(ql), f(kl), f(vl), f(lal[:, :, d]), S_ctx)
        ac, uc = rglru_coeffs(f(xrc), wa[d], ba[d], wx[d], bx[d], lam[d])
        hcd = linear_scan(ac, uc, h0)
        al, ul = rglru_coeffs(f(xrl), wa[d], ba[d], wx[d], bx[d], lam[d])
        hld = linear_scan(al, ul, hcd[:, -1])
        gla_c.append(f(oc)); gla_l.append(f(ol)); lru_c.append(f(hcd)); lru_l.append(f(hld))

    def assemble(o_gla, g, h_lru, zg, dtype):
        T = o_gla.shape[1]
        y_gla = rms_norm(o_gla, gla_g).reshape(B, T, -1) * jax.nn.silu(g.astype(f32))
        y_lru = h_lru * jax.nn.gelu(zg.astype(f32))
        return jnp.concatenate([y_gla, y_lru], axis=-1).astype(dtype) @ w_out

    y_l = assemble(gla_l[0] + gla_l[1], gl, lru_l[0] + lru_l[1], zgl, hl.dtype)
    y_c = None
    if need_ctx:
        y_c = assemble(gla_c[0] + gla_c[1], gc, lru_c[0] + lru_c[1], zgc, hc.dtype)
    return y_c, y_l


def setup_inputs(seed: int = 0) -> dict:
    key = jax.random.key(seed)
    ks = iter(jax.random.split(key, 40))
    f32 = jnp.float32

    def nrm(shape, scale):
        return jax.random.normal(next(ks), shape, f32) * scale

    blk = LRU_WIDTH // LRU_BLOCKS
    x = nrm((BATCH, SEQ, D_MODEL), 1.0)
    c = nrm((BATCH, D_MODEL), 1.0)
    ctx = nrm((BATCH, CTX_LEN, D_MODEL), 1.0)
    c_ctx = nrm((D_MODEL,), 1.0)
    ada_w = nrm((DEPTH, D_MODEL, 6 * D_MODEL), 0.5 * D_MODEL ** -0.5)
    ada_b = nrm((DEPTH, 6 * D_MODEL), 0.02)
    norm_g = 1.0 + nrm((DEPTH, 2, D_MODEL), 0.02)
    even_w_in = nrm((N_EVEN, D_MODEL, EVEN_IN), D_MODEL ** -0.5)
    even_w_out = nrm((N_EVEN, EVEN_MIX, D_MODEL), EVEN_MIX ** -0.5)
    diff_lam = nrm((N_EVEN, 4, DA_QK_DIM), 0.1)
    win_sink = nrm((N_EVEN, WA_Q_HEADS), 0.5)
    odd_w_in = nrm((N_ODD, D_MODEL, ODD_IN), D_MODEL ** -0.5)
    odd_w_out = nrm((N_ODD, ODD_MIX, D_MODEL), ODD_MIX ** -0.5)
    gla_gate_w = nrm((N_ODD, 2, GLA_GATE_RANK, GLA_HEADS * GLA_K_DIM), GLA_GATE_RANK ** -0.5)
    gla_gate_b = nrm((N_ODD, 2, GLA_HEADS * GLA_K_DIM), 0.1)
    gla_norm_g = 1.0 + nrm((N_ODD, GLA_V_DIM), 0.02)
    lru_conv_w = nrm((N_ODD, LRU_CONV, LRU_WIDTH), LRU_CONV ** -0.5)
    lru_conv_b = nrm((N_ODD, LRU_WIDTH), 0.02)
    lru_wa = nrm((N_ODD, 2, LRU_BLOCKS, blk, blk), blk ** -0.5)
    lru_ba = nrm((N_ODD, 2, LRU_WIDTH), 0.02)
    lru_wx = nrm((N_ODD, 2, LRU_BLOCKS, blk, blk), blk ** -0.5)
    lru_bx = nrm((N_ODD, 2, LRU_WIDTH), 0.02)
    u = jax.random.uniform(next(ks), (N_ODD, 2, LRU_WIDTH), f32, 0.9, 0.999)
    s = u ** (1.0 / LRU_C)
    lru_lam = jnp.log(s) - jnp.log1p(-s)
    ffn_w_in = nrm((DEPTH, D_MODEL, 2 * D_FF), D_MODEL ** -0.5)
    ffn_w_out = nrm((DEPTH, D_FF, D_MODEL), D_FF ** -0.5)
    final_g = 1.0 + nrm((D_MODEL,), 0.02)
    return {'x': x, 'c': c, 'ctx': ctx, 'c_ctx': c_ctx, 'ada_w': ada_w, 'ada_b': ada_b,
            'norm_g': norm_g, 'even_w_in': even_w_in, 'even_w_out': even_w_out,
            'diff_lam': diff_lam, 'win_sink': win_sink, 'odd_w_in': odd_w_in,
            'odd_w_out': odd_w_out, 'gla_gate_w': gla_gate_w, 'gla_gate_b': gla_gate_b,
            'gla_norm_g': gla_norm_g, 'lru_conv_w': lru_conv_w, 'lru_conv_b': lru_conv_b,
            'lru_wa': lru_wa, 'lru_ba': lru_ba, 'lru_wx': lru_wx, 'lru_bx': lru_bx,
            'lru_lam': lru_lam, 'ffn_w_in': ffn_w_in, 'ffn_w_out': ffn_w_out, 'final_g': final_g}


def reference(x, c, ctx, c_ctx, ada_w, ada_b, norm_g, even_w_in, even_w_out, diff_lam, win_sink,
              odd_w_in, odd_w_out, gla_gate_w, gla_gate_b, gla_norm_g, lru_conv_w, lru_conv_b,
              lru_wa, lru_ba, lru_wx, lru_bx, lru_lam, ffn_w_in, ffn_w_out, final_g):
    rope = rope_tables(x.shape[1])
    xl, xc = x, ctx
    cond_l = jax.nn.silu(c)[:, None, :]
    cond_c = jax.nn.silu(c_ctx)
    for li in range(DEPTH):
        last = li == DEPTH - 1
        sh1, sc1, g1, sh2, sc2, g2 = jnp.split(cond_l @ ada_w[li] + ada_b[li], 6, axis=-1)
        csh1, csc1, cg1, csh2, csc2, cg2 = jnp.split(cond_c @ ada_w[li] + ada_b[li], 6, axis=-1)
        hl = modulate(rms_norm(xl, norm_g[li, 0]), sh1, sc1)
        hc = modulate(rms_norm(xc, norm_g[li, 0]), csh1, csc1)
        if li % 2 == 0:
            e = li // 2
            lam_init = 0.8 - 0.6 * math.exp(-0.3 * li)
            yc, yl = even_mixer(hc, hl, even_w_in[e], even_w_out[e], diff_lam[e], win_sink[e],
                                lam_init, rope, not last)
        else:
            o = li // 2
            yc, yl = odd_mixer(hc, hl, odd_w_in[o], odd_w_out[o], gla_gate_w[o], gla_gate_b[o],
                               gla_norm_g[o], lru_conv_w[o], lru_conv_b[o], lru_wa[o], lru_ba[o],
                               lru_wx[o], lru_bx[o], lru_lam[o], not last)
        xl = xl + g1 * yl
        xl = xl + g2 * swiglu(modulate(rms_norm(xl, norm_g[li, 1]), sh2, sc2), ffn_w_in[li], ffn_w_out[li])
        if not last:
            xc = xc + cg1 * yc
            xc = xc + cg2 * swiglu(modulate(rms_norm(xc, norm_g[li, 1]), csh2, csc2),
                                   ffn_w_in[li], ffn_w_out[li])
    return rms_norm(xl, final_g)
```

```python
import functools
import math

import jax
import jax.numpy as jnp
from jax import lax
from jax.experimental import pallas as pl
from jax.experimental.pallas import tpu as pltpu

F32 = jnp.float32
BF16 = jnp.bfloat16

NORM_EPS = 1e-6
ROPE_THETA = 10000.0
ROPE_DIM = 64
GRID_W = 64
MASK_VALUE = -1e30

DA_HEADS = 4
DA_QK_DIM = 64
DA_V_DIM = 128
WA_Q_HEADS = 8
WA_KV_HEADS = 2
WA_GROUP = WA_Q_HEADS // WA_KV_HEADS
WA_HEAD_DIM = 64
WINDOW = 128

GLA_HEADS = 4
GLA_K_DIM = 64
GLA_V_DIM = 128
GLA_GATE_RANK = 16
GLA_GATE_NORM = 16.0
LRU_WIDTH = 512
LRU_BLOCKS = 8
LRU_CONV = 4
LRU_C = 8.0

LANES = 128
SUBLANES = 8
VMEM_LIMIT_BYTES = 56 * 1024 * 1024

GLA_CHUNK = 64
GLA_SUB = 16
EVEN_IN = 2304
ODD_IN_PAD = 2688


def _params(*sem):
    return pltpu.CompilerParams(dimension_semantics=sem, vmem_limit_bytes=VMEM_LIMIT_BYTES)


def _resident(block_shape, index_map):
    return pl.BlockSpec(block_shape, index_map, pipeline_mode=pl.Buffered(1))


def _softplus(x):
    return jnp.maximum(x, 0.0) + jnp.log1p(jnp.exp(-jnp.abs(x)))


def _silu(x):
    return x * jax.nn.sigmoid(x)


def _gelu_tanh(x):
    return 0.5 * x * (1.0 + jnp.tanh(math.sqrt(2.0 / math.pi) * (x + 0.044715 * (x * x * x))))


def _dot_nt(a, b):
    return lax.dot_general(a, b, (((1,), (1,)), ((), ())), preferred_element_type=F32)


def _dot_tn(a, b):
    return lax.dot_general(a, b, (((0,), (0,)), ((), ())), preferred_element_type=F32)


def _ada_kernel(cond_ref, w_ref, b_ref, o_ref):
    s = _silu(cond_ref[...])
    o_ref[...] = jnp.dot(s, w_ref[...], preferred_element_type=F32,
                         precision=lax.Precision.HIGHEST) + b_ref[...]


def _ada_call(cond, ada_w, ada_b):
    depth, d, n = ada_w.shape
    tn = 1536
    return pl.pallas_call(
        _ada_kernel,
        grid=(depth, n // tn),
        in_specs=[pl.BlockSpec((SUBLANES, d), lambda l, j: (0, 0)),
                  pl.BlockSpec((None, d, tn), lambda l, j: (l, 0, j)),
                  pl.BlockSpec((None, 1, tn), lambda l, j: (l, 0, j))],
        out_specs=pl.BlockSpec((None, SUBLANES, tn), lambda l, j: (l, 0, j)),
        out_shape=jax.ShapeDtypeStruct((depth, SUBLANES, n), F32),
        compiler_params=_params("arbitrary", "arbitrary"),
        name="ada_mod",
    )(cond, ada_w, ada_b.reshape(depth, 1, n))


def _norm_mod(x, g, shift, scale):
    ms = jnp.mean(x * x, axis=-1, keepdims=True)
    return (x * lax.rsqrt(ms + NORM_EPS) * g) * (1.0 + scale) + shift


def _in_proj_kernel(*refs, rope_groups, n_chunk):
    if rope_groups:
        x_ref, mod_ref, g_ref, w_ref, cs_ref, o_ref = refs
    else:
        x_ref, mod_ref, g_ref, w_ref, o_ref = refs
    x = x_ref[...]
    d = x.shape[-1]
    mod = mod_ref[...]
    hb = _norm_mod(x, g_ref[...], mod[:, 0:d], mod[:, d:2 * d]).astype(BF16)
    n = w_ref.shape[1]
    if rope_groups:
        cos = cs_ref[:, 0:LANES]
        sin = cs_ref[:, LANES:2 * LANES]
        lane = lax.broadcasted_iota(jnp.int32, (x.shape[0], LANES), 1)
        first_half = (lane % 32) < 16
    for n0 in range(0, n, n_chunk):
        acc = jnp.dot(hb, w_ref[:, n0:n0 + n_chunk], preferred_element_type=F32)
        if not rope_groups:
            o_ref[:, n0:n0 + n_chunk] = acc.astype(o_ref.dtype)
            continue
        for j in range(n_chunk // LANES):
            grp = acc[:, j * LANES:(j + 1) * LANES]
            if (n0 // LANES + j) in rope_groups:
                partner = jnp.where(first_half, pltpu.roll(grp, LANES - 16, 1), pltpu.roll(grp, 16, 1))
                grp = grp * cos + partner * sin
            c0 = n0 + j * LANES
            o_ref[:, c0:c0 + LANES] = grp.astype(o_ref.dtype)


def _in_proj_call(x, mod3, mod_row, g, w, cs, *, tm, n_chunk, rope_groups, out_dtype, name):
    b, t, d = x.shape
    n = w.shape[1]
    in_specs = [pl.BlockSpec((None, tm, d), lambda bi, i: (bi, i, 0)),
                pl.BlockSpec((None, 1, mod3.shape[-1]), lambda bi, i: (mod_row(bi), 0, 0)),
                pl.BlockSpec((1, d), lambda bi, i: (0, 0)),
                _resident((d, n), lambda bi, i: (0, 0))]
    args = [x, mod3, g.reshape(1, d), w]
    if rope_groups:
        in_specs.append(pl.BlockSpec((tm, 2 * LANES), lambda bi, i: (i, 0)))
        args.append(cs)
    return pl.pallas_call(
        functools.partial(_in_proj_kernel, rope_groups=rope_groups, n_chunk=n_chunk),
        grid=(b, t // tm),
        in_specs=in_specs,
        out_specs=pl.BlockSpec((None, tm, n), lambda bi, i: (bi, i, 0)),
        out_shape=jax.ShapeDtypeStruct((b, t, n), out_dtype),
        compiler_params=_params("parallel", "parallel"),
        name=name,
    )(*args)


def _diff_attn_kernel(*refs, tk, n_lat_chunks, lam_init):
    if n_lat_chunks:
        lam_ref, q_ref, kc_ref, vc_ref, kl_ref, vl_ref, o_ref = refs
    else:
        lam_ref, q_ref, kc_ref, vc_ref, o_ref = refs
    q = q_ref[...]
    tq = q.shape[0]
    lane = lax.broadcasted_iota(jnp.int32, q.shape, 1)
    zero = jnp.zeros_like(q)
    qs = jnp.concatenate([jnp.where(lane < DA_QK_DIM, q, zero), jnp.where(lane >= DA_QK_DIM, q, zero)], axis=0)

    def step(k, v, carry):
        m, l, acc = carry
        s = _dot_nt(qs, k)
        m_new = jnp.maximum(m, jnp.max(s, axis=-1, keepdims=True))
        alpha = jnp.exp(m - m_new)
        p = jnp.exp(s - m_new)
        l = alpha * l + jnp.sum(p, axis=-1, keepdims=True)
        acc = alpha * acc + jnp.dot(p.astype(BF16), v, preferred_element_type=F32)
        return m_new, l, acc

    carry = (jnp.full((2 * tq, 1), MASK_VALUE, F32), jnp.zeros((2 * tq, 1), F32),
             jnp.zeros((2 * tq, DA_V_DIM), F32))
    carry = step(kc_ref[...], vc_ref[...], carry)
    if n_lat_chunks:
        def body(i, c):
            r0 = pl.multiple_of(i * tk, tk)
            return step(kl_ref[pl.ds(r0, tk), :], vl_ref[pl.ds(r0, tk), :], c)
        carry = lax.fori_loop(0, n_lat_chunks, body, carry)
    _, l, acc = carry
    o = acc / l
    lv = lam_ref[...]
    lam = (jnp.exp(jnp.sum(lv[0:1] * lv[1:2], axis=-1, keepdims=True))
           - jnp.exp(jnp.sum(lv[2:3] * lv[3:4], axis=-1, keepdims=True)) + lam_init)
    w = o[0:tq] - lam * o[tq:2 * tq]
    y = w * lax.rsqrt(jnp.mean(w * w, axis=-1, keepdims=True) + NORM_EPS) * (1.0 - lam_init)
    o_ref[...] = y.astype(o_ref.dtype)


def _diff_attn_call(lam_vec, pq, pc, pl_kv, *, tq, tk, lam_init, name):
    b, t, _ = pq.shape
    lc = pc.shape[1]
    kcol, vcol = DA_HEADS, 2 * DA_HEADS
    in_specs = [pl.BlockSpec((4, DA_QK_DIM), lambda bi, h, i: (0, 0)),
                pl.BlockSpec((None, tq, LANES), lambda bi, h, i: (bi, i, h)),
                pl.BlockSpec((None, lc, LANES), lambda bi, h, i: (bi, 0, kcol + h)),
                pl.BlockSpec((None, lc, LANES), lambda bi, h, i: (bi, 0, vcol + h))]
    args = [lam_vec, pq, pc, pc]
    n_lat = 0
    if pl_kv is not None:
        l = pl_kv.shape[1]
        n_lat = l // tk
        in_specs += [pl.BlockSpec((None, l, LANES), lambda bi, h, i: (bi, 0, kcol + h)),
                     pl.BlockSpec((None, l, LANES), lambda bi, h, i: (bi, 0, vcol + h))]
        args += [pl_kv, pl_kv]
    return pl.pallas_call(
        functools.partial(_diff_attn_kernel, tk=tk, n_lat_chunks=n_lat, lam_init=lam_init),
        grid=(b, DA_HEADS, t // tq),
        in_specs=in_specs,
        out_specs=pl.BlockSpec((None, tq, LANES), lambda bi, h, i: (bi, i, h)),
        out_shape=jax.ShapeDtypeStruct((b, t, DA_HEADS * DA_V_DIM), BF16),
        compiler_params=_params("parallel", "parallel", "arbitrary"),
        name=name,
    )(*args)


def _win_attn_kernel(*refs, tq, span, has_win):
    if has_win:
        sink_ref, q_ref, kc_ref, vc_ref, kl_ref, vl_ref, o_ref = refs
    else:
        sink_ref, q_ref, kc_ref, vc_ref, o_ref = refs
    q = q_ref[...]
    kc = kc_ref[...]
    vc = vc_ref[...]
    if has_win:
        l = kl_ref.shape[0]
        q0 = pl.program_id(1) * tq
        start = pl.multiple_of(jnp.clip(q0 - WINDOW, 0, l - span), WINDOW)
        kw = kl_ref[pl.ds(start, span), :]
        vw = vl_ref[pl.ds(start, span), :]
        qpos = q0 + lax.broadcasted_iota(jnp.int32, (tq, span), 0)
        kpos = start + lax.broadcasted_iota(jnp.int32, (tq, span), 1)
        valid = jnp.abs(kpos - qpos) <= WINDOW
    for hk in range(WA_KV_HEADS):
        ks = slice(hk * WA_HEAD_DIM, (hk + 1) * WA_HEAD_DIM)
        kc_h, vc_h = kc[:, ks], vc[:, ks]
        if has_win:
            kw_h, vw_h = kw[:, ks], vw[:, ks]
        for g in range(WA_GROUP):
            h = hk * WA_GROUP + g
            hs = slice(h * WA_HEAD_DIM, (h + 1) * WA_HEAD_DIM)
            q_h = q[:, hs]
            sink = sink_ref[h]
            s_c = _dot_nt(q_h, kc_h)
            m = jnp.maximum(jnp.max(s_c, axis=-1, keepdims=True), sink)
            if has_win:
                s_w = jnp.where(valid, _dot_nt(q_h, kw_h), MASK_VALUE)
                m = jnp.maximum(m, jnp.max(s_w, axis=-1, keepdims=True))
            p_c = jnp.exp(s_c - m)
            den = jnp.sum(p_c, axis=-1, keepdims=True) + jnp.exp(sink - m)
            num = jnp.dot(p_c.astype(BF16), vc_h, preferred_element_type=F32)
            if has_win:
                p_w = jnp.where(valid, jnp.exp(s_w - m), 0.0)
                den = den + jnp.sum(p_w, axis=-1, keepdims=True)
                num = num + jnp.dot(p_w.astype(BF16), vw_h, preferred_element_type=F32)
            o_ref[:, hs] = (num / den).astype(o_ref.dtype)


def _win_attn_call(sink, pq, pc, pl_kv, *, tq, name):
    b, t, _ = pq.shape
    lc = pc.shape[1]
    qcol = 3
    kcol, vcol = 16, 17
    in_specs = [pl.BlockSpec(memory_space=pltpu.SMEM),
                pl.BlockSpec((None, tq, 4 * LANES), lambda bi, i: (bi, i, qcol)),
                pl.BlockSpec((None, lc, LANES), lambda bi, i: (bi, 0, kcol)),
                pl.BlockSpec((None, lc, LANES), lambda bi, i: (bi, 0, vcol))]
    args = [sink, pq, pc, pc]
    has_win = pl_kv is not None
    if has_win:
        l = pl_kv.shape[1]
        in_specs += [pl.BlockSpec((None, l, LANES), lambda bi, i: (bi, 0, kcol)),
                     pl.BlockSpec((None, l, LANES), lambda bi, i: (bi, 0, vcol))]
        args += [pl_kv, pl_kv]
    return pl.pallas_call(
        functools.partial(_win_attn_kernel, tq=tq, span=tq + 2 * WINDOW, has_win=has_win),
        grid=(b, t // tq),
        in_specs=in_specs,
        out_specs=pl.BlockSpec((None, tq, 4 * LANES), lambda bi, i: (bi, i, 0)),
        out_shape=jax.ShapeDtypeStruct((b, t, WA_Q_HEADS * WA_HEAD_DIM), BF16),
        compiler_params=_params("parallel", "arbitrary"),
        name=name,
    )(*args)


def _gla_chunk(q, k, v, g, state, rev):
    c, w = q.shape
    nsub = c // GLA_SUB
    row = lax.broadcasted_iota(jnp.int32, (c, w), 0)
    rb = row % GLA_SUB
    bl = g
    s = 1
    while s < GLA_SUB:
        if rev:
            bl = bl + jnp.where(rb < GLA_SUB - s, pltpu.roll(bl, c - s, 0), 0.0)
        else:
            bl = bl + jnp.where(rb >= s, pltpu.roll(bl, s, 0), 0.0)
        s *= 2
    order = list(range(nsub))[::-1] if rev else list(range(nsub))
    rank = {blk: n for n, blk in enumerate(order)}

    def rows(a, i):
        return a[i * GLA_SUB:(i + 1) * GLA_SUB]

    tot = {}
    for i in range(nsub):
        r = i * GLA_SUB if rev else i * GLA_SUB + GLA_SUB - 1
        tot[i] = bl[r:r + 1]
    pre = {}
    run = jnp.zeros((1, w), F32)
    for i in order:
        pre[i] = run
        run = run + tot[i]
    total = run

    def cat(fn):
        return jnp.concatenate([fn(i) for i in range(nsub)], axis=0)

    zeros = jnp.zeros((GLA_SUB, w), F32)
    b_full = cat(lambda i: rows(bl, i) + pre[i])
    qe = q * jnp.exp(b_full)
    kdec = k * jnp.exp(total - b_full)
    qd = q * jnp.exp(bl)
    kd = k * jnp.exp(-bl)
    ko = cat(lambda i: rows(k, i) * jnp.exp(tot[i] - rows(bl, i)))

    def stack_keys(x):
        r = lax.broadcasted_iota(jnp.int32, (GLA_HEADS * c, w), 0) // c
        ln = lax.broadcasted_iota(jnp.int32, (GLA_HEADS * c, w), 1) // GLA_K_DIM
        return jnp.where(r == ln, jnp.concatenate([x] * GLA_HEADS, axis=0), 0.0).astype(BF16)

    qi = lax.broadcasted_iota(jnp.int32, (c, GLA_HEADS * c), 0)
    kj = lax.broadcasted_iota(jnp.int32, (c, GLA_HEADS * c), 1) % c
    same = (qi // GLA_SUB) == (kj // GLA_SUB)
    tri = (qi <= kj) if rev else (qi >= kj)
    a = jnp.where(same & tri, _dot_nt(qd.astype(BF16), stack_keys(kd)), 0.0)
    for j in order[:-1]:
        off = pre[j] + tot[j]
        qo = cat(lambda i: rows(q, i) * jnp.exp(rows(b_full, i) - off) if rank[i] > rank[j] else zeros)
        koj = cat(lambda i: rows(ko, i) if i == j else zeros)
        a = a + _dot_nt(qo.astype(BF16), stack_keys(koj))

    vr = lax.broadcasted_iota(jnp.int32, (GLA_HEADS * c, v.shape[1]), 0) // c
    vl = lax.broadcasted_iota(jnp.int32, (GLA_HEADS * c, v.shape[1]), 1) // GLA_V_DIM
    vst = jnp.where(vr == vl, jnp.concatenate([v] * GLA_HEADS, axis=0), 0.0).astype(BF16)
    o = _dot_nt(qe.astype(BF16), state.astype(BF16)) + jnp.dot(a.astype(BF16), vst, preferred_element_type=F32)

    sr = lax.broadcasted_iota(jnp.int32, state.shape, 0) // GLA_V_DIM
    sl = lax.broadcasted_iota(jnp.int32, state.shape, 1) // GLA_K_DIM
    upd = _dot_tn(v.astype(BF16), kdec.astype(BF16))
    new_state = state * jnp.exp(total) + jnp.where(sr == sl, upd, 0.0)
    return o, new_state


def _gla_kernel(q_ref, k_ref, v_ref, lr_ref, gw_ref, gb_ref, s0_ref, o_ref, sT_ref, s_scr, *, rev, nc):
    i = pl.program_id(1)

    @pl.when(i == 0)
    def _():
        s_scr[...] = s0_ref[...]

    def body(n, carry):
        r0 = pl.multiple_of(((nc - 1 - n) if rev else n) * GLA_CHUNK, GLA_CHUNK)
        sl = pl.ds(r0, GLA_CHUNK)
        logit = jnp.dot(lr_ref[sl, :], gw_ref[...], preferred_element_type=F32) + gb_ref[...]
        g = (jnp.minimum(logit, 0.0) - jnp.log1p(jnp.exp(-jnp.abs(logit)))) * (1.0 / GLA_GATE_NORM)
        o, new_state = _gla_chunk(q_ref[sl, :], k_ref[sl, :], v_ref[sl, :], g, s_scr[...], rev)
        o_ref[sl, :] = o
        s_scr[...] = new_state
        return carry

    lax.fori_loop(0, nc, body, 0)

    @pl.when(i == pl.num_programs(1) - 1)
    def _():
        sT_ref[...] = s_scr[...]


def _gla_call(p, gw, gb, s0, *, tb, rev, name):
    b, t, _ = p.shape
    nb = t // tb
    hk = GLA_HEADS * GLA_K_DIM
    hv = GLA_HEADS * GLA_V_DIM

    def tblk(i):
        return (nb - 1 - i) if rev else i

    return pl.pallas_call(
        functools.partial(_gla_kernel, rev=rev, nc=tb // GLA_CHUNK),
        grid=(b, nb),
        in_specs=[pl.BlockSpec((None, tb, hk), lambda bi, i: (bi, tblk(i), 0)),
                  pl.BlockSpec((None, tb, hk), lambda bi, i: (bi, tblk(i), 1)),
                  pl.BlockSpec((None, tb, hv), lambda bi, i: (bi, tblk(i), 1)),
                  pl.BlockSpec((None, tb, LANES), lambda bi, i: (bi, tblk(i), 2560 // LANES)),
                  pl.BlockSpec((LANES, hk), lambda bi, i: (0, 0)),
                  pl.BlockSpec((1, hk), lambda bi, i: (0, 0)),
                  pl.BlockSpec((None, hv, hk), lambda bi, i: (bi, 0, 0))],
        out_specs=[pl.BlockSpec((None, tb, hv), lambda bi, i: (bi, tblk(i), 0)),
                   pl.BlockSpec((None, hv, hk), lambda bi, i: (bi, 0, 0))],
        out_shape=[jax.ShapeDtypeStruct((b, t, hv), F32), jax.ShapeDtypeStruct((b, hv, hk), F32)],
        scratch_shapes=[pltpu.VMEM((hv, hk), F32)],
        compiler_params=_params("parallel", "arbitrary"),
        name=name,
    )(p, p, p, p, gw, gb, s0)


def _lru_kernel(zx_ref, zp_ref, zn_ref, cw_ref, cb_ref, wg_ref, bg_ref, lam_ref, h0_ref,
                h_ref, hT_ref, carry_scr, *, rev, nb):
    i = pl.program_id(1)
    blk = (nb - 1 - i) if rev else i

    @pl.when(i == 0)
    def _():
        carry_scr[...] = h0_ref[...]

    x = zx_ref[...]
    tb, w = x.shape
    row = lax.broadcasted_iota(jnp.int32, (tb, w), 0)
    pm = jnp.where(blk > 0, zp_ref[SUBLANES - 1:SUBLANES, :], 0.0)
    n0 = jnp.where(blk < nb - 1, zn_ref[0:1, :], 0.0)
    n1 = jnp.where(blk < nb - 1, zn_ref[1:2, :], 0.0)
    xm1 = jnp.where(row == 0, pm, pltpu.roll(x, 1, 0))
    xp1 = jnp.where(row == tb - 1, n0, pltpu.roll(x, tb - 1, 0))
    xp2 = jnp.where(row == tb - 1, n1, jnp.where(row == tb - 2, n0, pltpu.roll(x, tb - 2, 0)))
    cw = cw_ref[...]
    xr = xm1 * cw[0:1] + x * cw[1:2] + xp1 * cw[2:3] + xp2 * cw[3:4] + cb_ref[...]

    gm = jnp.dot(xr.astype(BF16), wg_ref[...], preferred_element_type=F32) + bg_ref[...]
    r = jax.nn.sigmoid(gm[:, 0:w])
    ig = jax.nn.sigmoid(gm[:, w:2 * w])
    log_a = (-LRU_C * r) * _softplus(-lam_ref[...])
    a = jnp.exp(log_a)
    u = jnp.sqrt(-jnp.tanh(log_a) * (a * a + 1.0)) * (ig * xr)

    s = 1
    while s < tb:
        if rev:
            keep = row < tb - s
            a_sh = jnp.where(keep, pltpu.roll(a, tb - s, 0), 1.0)
            u_sh = jnp.where(keep, pltpu.roll(u, tb - s, 0), 0.0)
        else:
            keep = row >= s
            a_sh = jnp.where(keep, pltpu.roll(a, s, 0), 1.0)
            u_sh = jnp.where(keep, pltpu.roll(u, s, 0), 0.0)
        u = a * u_sh + u
        a = a * a_sh
        s *= 2
    h = u + a * carry_scr[0:1, :]
    h_ref[...] = h
    last = h[0:1] if rev else h[tb - 1:tb]
    carry_scr[...] = jnp.broadcast_to(last, carry_scr.shape)

    @pl.when(i == nb - 1)
    def _():
        hT_ref[...] = carry_scr[...]


def _lru_call(p, cw, cb, wg, bg, lam, h0, *, tb, rev, name):
    b, t, _ = p.shape
    nb = t // tb
    w = LRU_WIDTH
    zcol = 2048 // w
    per = tb // SUBLANES

    def tblk(i):
        return (nb - 1 - i) if rev else i

    return pl.pallas_call(
        functools.partial(_lru_kernel, rev=rev, nb=nb),
        grid=(b, nb),
        in_specs=[pl.BlockSpec((None, tb, w), lambda bi, i: (bi, tblk(i), zcol)),
                  pl.BlockSpec((None, SUBLANES, w), lambda bi, i: (bi, jnp.maximum(tblk(i) * per - 1, 0), zcol)),
                  pl.BlockSpec((None, SUBLANES, w),
                               lambda bi, i: (bi, jnp.minimum((tblk(i) + 1) * per, nb * per - 1), zcol)),
                  pl.BlockSpec((LRU_CONV, w), lambda bi, i: (0, 0)),
                  pl.BlockSpec((1, w), lambda bi, i: (0, 0)),
                  pl.BlockSpec((w, 2 * w), lambda bi, i: (0, 0)),
                  pl.BlockSpec((1, 2 * w), lambda bi, i: (0, 0)),
                  pl.BlockSpec((1, w), lambda bi, i: (0, 0)),
                  pl.BlockSpec((None, SUBLANES, w), lambda bi, i: (bi, 0, 0))],
        out_specs=[pl.BlockSpec((None, tb, w), lambda bi, i: (bi, tblk(i), 0)),
                   pl.BlockSpec((None, SUBLANES, w), lambda bi, i: (bi, 0, 0))],
        out_shape=[jax.ShapeDtypeStruct((b, t, w), F32), jax.ShapeDtypeStruct((b, SUBLANES, w), F32)],
        scratch_shapes=[pltpu.VMEM((SUBLANES, w), F32)],
        compiler_params=_params("parallel", "arbitrary"),
        name=name,
    )(p, p, p, cw, cb, wg, bg, lam, h0)


def _out_ffn_kernel(*refs, kind, final, f_chunk):
    if kind == 0:
        x_ref, ma_ref, mb_ref = refs[0:3]
        rest = refs[3:]
    else:
        x_ref, of_ref, ob_ref, hf_ref, hb_ref, gg_ref, zg_ref, glag_ref = refs[0:8]
        rest = refs[8:]
    if final:
        mod_ref, g2_ref, wo_ref, wi_ref, wf_ref, fg_ref, o_ref = rest
    else:
        mod_ref, g2_ref, wo_ref, wi_ref, wf_ref, o_ref = rest
    x = x_ref[...]
    d = x.shape[-1]
    if kind == 0:
        mix_a = ma_ref[...]
        mix_b = mb_ref[...]
    else:
        og = of_ref[...] + ob_ref[...]
        gg = gg_ref[...]
        parts = []
        for h in range(GLA_HEADS):
            hs = slice(h * GLA_V_DIM, (h + 1) * GLA_V_DIM)
            seg = og[:, hs]
            nrm = seg * lax.rsqrt(jnp.mean(seg * seg, axis=-1, keepdims=True) + NORM_EPS) * glag_ref[...]
            parts.append(nrm * _silu(gg[:, hs]))
        mix_a = jnp.concatenate(parts, axis=-1).astype(BF16)
        mix_b = ((hf_ref[...] + hb_ref[...]) * _gelu_tanh(zg_ref[...])).astype(BF16)
    wa = mix_a.shape[-1]
    y = (jnp.dot(mix_a, wo_ref[0:wa, :], preferred_element_type=F32)
         + jnp.dot(mix_b, wo_ref[wa:, :], preferred_element_type=F32))
    mod = mod_ref[...]
    x1 = x + mod[:, 2 * d:3 * d] * y
    h2 = _norm_mod(x1, g2_ref[...], mod[:, 3 * d:4 * d], mod[:, 4 * d:5 * d]).astype(BF16)
    f = wf_ref.shape[0]
    acc = None
    for f0 in range(0, f, f_chunk):
        gate = jnp.dot(h2, wi_ref[:, f0:f0 + f_chunk], preferred_element_type=F32)
        up = jnp.dot(h2, wi_ref[:, f + f0:f + f0 + f_chunk], preferred_element_type=F32)
        act = (_silu(gate) * up).astype(BF16)
        part = jnp.dot(act, wf_ref[f0:f0 + f_chunk, :], preferred_element_type=F32)
        acc = part if acc is None else acc + part
    x2 = x1 + mod[:, 5 * d:6 * d] * acc
    if final:
        ms = jnp.mean(x2 * x2, axis=-1, keepdims=True)
        x2 = x2 * lax.rsqrt(ms + NORM_EPS) * fg_ref[...]
    o_ref[...] = x2


def _out_ffn_call(x, mix, mod3, mod_row, g2, wo, wi, wf, final_g, *, kind, tm, f_chunk, name):
    b, t, d = x.shape
    f = wf.shape[0]

    def tile(width, col=0):
        return pl.BlockSpec((None, tm, width), lambda bi, i: (bi, i, col))

    in_specs = [tile(d)]
    args = [x]
    if kind == 0:
        ma, mb = mix
        in_specs += [tile(ma.shape[-1]), tile(mb.shape[-1])]
        args += [ma, mb]
    else:
        o_f, o_b, h_f, h_b, p, gla_g = mix
        hv = GLA_HEADS * GLA_V_DIM
        in_specs += [tile(hv), tile(hv), tile(LRU_WIDTH), tile(LRU_WIDTH),
                     tile(hv, 1024 // hv), tile(LRU_WIDTH, 1536 // LRU_WIDTH),
                     pl.BlockSpec((1, GLA_V_DIM), lambda bi, i: (0, 0))]
        args += [o_f, o_b, h_f, h_b, p, p, gla_g.reshape(1, GLA_V_DIM)]
    in_specs += [pl.BlockSpec((None, 1, mod3.shape[-1]), lambda bi, i: (mod_row(bi), 0, 0)),
                 pl.BlockSpec((1, d), lambda bi, i: (0, 0)),
                 _resident(wo.shape, lambda bi, i: (0, 0)),
                 _resident(wi.shape, lambda bi, i: (0, 0)),
                 _resident(wf.shape, lambda bi, i: (0, 0))]
    args += [mod3, g2.reshape(1, d), wo, wi, wf]
    final = final_g is not None
    if final:
        in_specs.append(pl.BlockSpec((1, d), lambda bi, i: (0, 0)))
        args.append(final_g.reshape(1, d))
    return pl.pallas_call(
        functools.partial(_out_ffn_kernel, kind=kind, final=final, f_chunk=f_chunk),
        grid=(b, t // tm),
        in_specs=in_specs,
        out_specs=pl.BlockSpec((None, tm, d), lambda bi, i: (bi, i, 0)),
        out_shape=jax.ShapeDtypeStruct((b, t, d), F32),
        compiler_params=_params("parallel", "parallel"),
        name=name,
    )(*args)


def _rope_table(n_tokens):
    n_rows = n_tokens // GRID_W
    row = jnp.broadcast_to(jnp.arange(n_rows)[:, None], (n_rows, GRID_W)).reshape(-1)
    col = jnp.broadcast_to(jnp.arange(GRID_W)[None, :], (n_rows, GRID_W)).reshape(-1)
    axis_dim = ROPE_DIM // 2
    inv_freq = ROPE_THETA ** (-jnp.arange(0, axis_dim, 2, dtype=F32) / axis_dim)
    ang_r = row.astype(F32)[:, None] * inv_freq
    ang_c = col.astype(F32)[:, None] * inv_freq
    cr, sr, cc, sc = jnp.cos(ang_r), jnp.sin(ang_r), jnp.cos(ang_c), jnp.sin(ang_c)
    cos = jnp.concatenate([cr, cr, cc, cc], axis=-1)
    sin = jnp.concatenate([-sr, sr, -sc, sc], axis=-1)
    reps = LANES // ROPE_DIM
    return jnp.concatenate([jnp.tile(cos, (1, reps)), jnp.tile(sin, (1, reps))], axis=-1)


def _even_w_in(w):
    qa, ka, va, qb, kb, vb = jnp.split(w, (512, 1024, 1536, 2048, 2176), axis=-1)
    return jnp.concatenate([qa * DA_QK_DIM ** -0.5, ka, va, qb * WA_HEAD_DIM ** -0.5, kb, vb], axis=-1).astype(BF16)


def _odd_w_in(w):
    q, k, v, g, lr, zg, zx = jnp.split(w, (256, 512, 1024, 1536, 1568, 2080), axis=-1)
    pad = jnp.zeros((w.shape[0], LANES - lr.shape[1]), w.dtype)
    return jnp.concatenate([q * GLA_K_DIM ** -0.5, k, v, g, zg, zx, lr, pad], axis=-1).astype(BF16)


def _block_diag(w):
    n, c, d = w.shape
    eye = jnp.eye(n, dtype=w.dtype)
    return (eye[:, None, :, None] * w[:, :, None, :]).reshape(n * c, n * d)


def _tile_rows(n, cap):
    t = min(n, cap)
    while n % t:
        t //= 2
    return t


def kernel(x, c, ctx, c_ctx, ada_w, ada_b, norm_g, even_w_in, even_w_out, diff_lam, win_sink, odd_w_in, odd_w_out, gla_gate_w, gla_gate_b, gla_norm_g, lru_conv_w, lru_conv_b, lru_wa, lru_ba, lru_wx, lru_bx, lru_lam, ffn_w_in, ffn_w_out, final_g):
    b, l, d = x.shape
    lc = ctx.shape[1]
    depth = ada_w.shape[0]
    assert b < SUBLANES and d == 1024

    cond = jnp.zeros((SUBLANES, d), F32).at[0:b].set(c).at[b].set(c_ctx)
    mod = _ada_call(cond, ada_w, ada_b)
    cs = _rope_table(l)

    xl = x
    xc = ctx.reshape(1, b * lc, d)
    tm_l = _tile_rows(l, 512)
    tm_c = _tile_rows(b * lc, 512)
    lat_row = lambda bi: bi
    ctx_row = lambda bi: b

    for li in range(depth):
        last = li == depth - 1
        mod3 = mod[li].reshape(SUBLANES, 1, 6 * d)
        wf_in = ffn_w_in[li].astype(BF16)
        wf_out = ffn_w_out[li].astype(BF16)
        fg = final_g if last else None
        f_chunk = wf_out.shape[0] // 2
        if li % 2 == 0:
            e = li // 2
            lam_init = 0.8 - 0.6 * math.exp(-0.3 * li)
            w_in = _even_w_in(even_w_in[e])
            w_out = even_w_out[e].astype(BF16)
            rope_groups = frozenset(range(0, 8)) | frozenset(range(12, 17))
            p_l = _in_proj_call(xl, mod3, lat_row, norm_g[li, 0], w_in, cs, tm=tm_l, n_chunk=768,
                                rope_groups=rope_groups, out_dtype=BF16, name=f"in_proj_l{li}")
            p_c = _in_proj_call(xc, mod3, ctx_row, norm_g[li, 0], w_in, None, tm=tm_c, n_chunk=768,
                                rope_groups=None, out_dtype=BF16, name=f"in_proj_c{li}")
            p_c = p_c.reshape(b, lc, EVEN_IN)
            tq = _tile_rows(l, 256)
            a_l = _diff_attn_call(diff_lam[e], p_l, p_c, p_l, tq=tq, tk=_tile_rows(l, 512),
                                  lam_init=lam_init, name=f"diff_attn_l{li}")
            b_l = _win_attn_call(win_sink[e], p_l, p_c, p_l, tq=tq, name=f"win_attn_l{li}")
            xl = _out_ffn_call(xl, (a_l, b_l), mod3, lat_row, norm_g[li, 1], w_out, wf_in, wf_out, fg,
                               kind=0, tm=tm_l, f_chunk=f_chunk, name=f"out_ffn_l{li}")
            if not last:
                a_c = _diff_attn_call(diff_lam[e], p_c, p_c, None, tq=lc, tk=lc,
                                      lam_init=lam_init, name=f"diff_attn_c{li}")
                b_c = _win_attn_call(win_sink[e], p_c, p_c, None, tq=lc, name=f"win_attn_c{li}")
                mix_c = (a_c.reshape(1, b * lc, -1), b_c.reshape(1, b * lc, -1))
                xc = _out_ffn_call(xc, mix_c, mod3, ctx_row, norm_g[li, 1], w_out, wf_in, wf_out, None,
                                   kind=0, tm=tm_c, f_chunk=f_chunk, name=f"out_ffn_c{li}")
        else:
            o = li // 2
            w_in = _odd_w_in(odd_w_in[o])
            w_out = odd_w_out[o].astype(BF16)
            p_l = _in_proj_call(xl, mod3, lat_row, norm_g[li, 0], w_in, None, tm=tm_l, n_chunk=896,
                                rope_groups=None, out_dtype=F32, name=f"in_proj_l{li}")
            p_c = _in_proj_call(xc, mod3, ctx_row, norm_g[li, 0], w_in, None, tm=tm_c, n_chunk=896,
                                rope_groups=None, out_dtype=F32, name=f"in_proj_c{li}")
            p_c = p_c.reshape(b, lc, ODD_IN_PAD)
            hk = GLA_HEADS * GLA_K_DIM
            hv = GLA_HEADS * GLA_V_DIM
            s0 = jnp.zeros((b, hv, hk), F32)
            h0 = jnp.zeros((b, SUBLANES, LRU_WIDTH), F32)
            cw = lru_conv_w[o]
            cb = lru_conv_b[o].reshape(1, LRU_WIDTH)
            outs = []
            for dr in range(2):
                rev = dr == 1
                gw = jnp.zeros((LANES, hk), F32).at[dr * GLA_GATE_RANK:(dr + 1) * GLA_GATE_RANK].set(gla_gate_w[o, dr])
                gb = gla_gate_b[o, dr].reshape(1, hk)
                o_c, s_c = _gla_call(p_c, gw, gb, s0, tb=_tile_rows(lc, 256), rev=rev, name=f"gla_c{li}d{dr}")
                o_l, _ = _gla_call(p_l, gw, gb, s_c, tb=_tile_rows(l, 512), rev=rev, name=f"gla_l{li}d{dr}")
                wg = jnp.concatenate([_block_diag(lru_wa[o, dr]), _block_diag(lru_wx[o, dr])], axis=-1).astype(BF16)
                bg = jnp.concatenate([lru_ba[o, dr], lru_bx[o, dr]]).reshape(1, 2 * LRU_WIDTH)
                lam = lru_lam[o, dr].reshape(1, LRU_WIDTH)
                h_c, hT = _lru_call(p_c, cw, cb, wg, bg, lam, h0, tb=_tile_rows(lc, 256), rev=rev,
                                    name=f"lru_c{li}d{dr}")
                h_l, _ = _lru_call(p_l, cw, cb, wg, bg, lam, hT, tb=_tile_rows(l, 256), rev=rev,
                                   name=f"lru_l{li}d{dr}")
                outs.append((o_c, o_l, h_c, h_l))
            mix_l = (outs[0][1], outs[1][1], outs[0][3], outs[1][3], p_l, gla_norm_g[o])
            xl = _out_ffn_call(xl, mix_l, mod3, lat_row, norm_g[li, 1], w_out, wf_in, wf_out, fg,
                               kind=1, tm=tm_l, f_chunk=f_chunk, name=f"out_ffn_l{li}")
            if not last:
                flat = lambda a: a.reshape(1, b * lc, a.shape[-1])
                mix_c = (flat(outs[0][0]), flat(outs[1][0]), flat(outs[0][2]), flat(outs[1][2]),
                         flat(p_c), gla_norm_g[o])
                xc = _out_ffn_call(xc, mix_c, mod3, ctx_row, norm_g[li, 1], w_out, wf_in, wf_out, None,
                                   kind=1, tm=tm_c, f_chunk=f_chunk, name=f"out_ffn_c{li}")
    return xl
```

```python
import functools
import math

import jax
import jax.numpy as jnp
from jax import lax
from jax.experimental import pallas as pl
from jax.experimental.pallas import tpu as pltpu

F32 = jnp.float32
BF16 = jnp.bfloat16

NORM_EPS = 1e-6
ROPE_THETA = 10000.0
ROPE_DIM = 64
GRID_W = 64
MASK_VALUE = -1e30

DA_HEADS = 4
DA_QK_DIM = 64
DA_V_DIM = 128
WA_Q_HEADS = 8
WA_KV_HEADS = 2
WA_GROUP = WA_Q_HEADS // WA_KV_HEADS
WA_HEAD_DIM = 64
WINDOW = 128

GLA_HEADS = 4
GLA_K_DIM = 64
GLA_V_DIM = 128
GLA_GATE_RANK = 16
GLA_GATE_NORM = 16.0
LRU_WIDTH = 512
LRU_BLOCKS = 8
LRU_CONV = 4
LRU_C = 8.0

LANES = 128
SUBLANES = 8
VMEM_LIMIT_BYTES = 56 * 1024 * 1024

GLA_CHUNK = 64
GLA_SUB = 16
EVEN_IN = 2304
ODD_IN_PAD = 2688


def _params(*sem):
    return pltpu.CompilerParams(dimension_semantics=sem, vmem_limit_bytes=VMEM_LIMIT_BYTES)


def _resident(block_shape, index_map):
    return pl.BlockSpec(block_shape, index_map, pipeline_mode=pl.Buffered(1))


def _softplus(x):
    return jnp.maximum(x, 0.0) + jnp.log1p(jnp.exp(-jnp.abs(x)))


def _silu(x):
    return x * jax.nn.sigmoid(x)


def _gelu_tanh(x):
    return 0.5 * x * (1.0 + jnp.tanh(math.sqrt(2.0 / math.pi) * (x + 0.044715 * (x * x * x))))


def _dot_nt(a, b):
    return lax.dot_general(a, b, (((1,), (1,)), ((), ())), preferred_element_type=F32)


def _dot_tn(a, b):
    return lax.dot_general(a, b, (((0,), (0,)), ((), ())), preferred_element_type=F32)


def _ada_kernel(cond_ref, w_ref, b_ref, o_ref):
    s = _silu(cond_ref[...])
    o_ref[...] = jnp.dot(s, w_ref[...], preferred_element_type=F32,
                         precision=lax.Precision.HIGHEST) + b_ref[...]


def _ada_call(cond, ada_w, ada_b):
    depth, d, n = ada_w.shape
    tn = 1536
    return pl.pallas_call(
        _ada_kernel,
        grid=(depth, n // tn),
        in_specs=[pl.BlockSpec((SUBLANES, d), lambda l, j: (0, 0)),
                  pl.BlockSpec((None, d, tn), lambda l, j: (l, 0, j)),
                  pl.BlockSpec((None, 1, tn), lambda l, j: (l, 0, j))],
        out_specs=pl.BlockSpec((None, SUBLANES, tn), lambda l, j: (l, 0, j)),
        out_shape=jax.ShapeDtypeStruct((depth, SUBLANES, n), F32),
        compiler_params=_params("arbitrary", "arbitrary"),
        name="ada_mod",
    )(cond, ada_w, ada_b.reshape(depth, 1, n))


def _norm_mod(x, g, shift, scale):
    ms = jnp.mean(x * x, axis=-1, keepdims=True)
    return (x * lax.rsqrt(ms + NORM_EPS) * g) * (1.0 + scale) + shift


def _in_proj_kernel(*refs, rope_groups, n_chunk):
    if rope_groups:
        x_ref, mod_ref, g_ref, w_ref, cs_ref, o_ref = refs
    else:
        x_ref, mod_ref, g_ref, w_ref, o_ref = refs
    x = x_ref[...]
    d = x.shape[-1]
    mod = mod_ref[...]
    hb = _norm_mod(x, g_ref[...], mod[:, 0:d], mod[:, d:2 * d]).astype(BF16)
    n = w_ref.shape[1]
    if rope_groups:
        cos = cs_ref[:, 0:LANES]
        sin = cs_ref[:, LANES:2 * LANES]
        lane = lax.broadcasted_iota(jnp.int32, (x.shape[0], LANES), 1)
        first_half = (lane % 32) < 16
    for n0 in range(0, n, n_chunk):
        acc = jnp.dot(hb, w_ref[:, n0:n0 + n_chunk], preferred_element_type=F32)
        if not rope_groups:
            o_ref[:, n0:n0 + n_chunk] = acc.astype(o_ref.dtype)
            continue
        for j in range(n_chunk // LANES):
            grp = acc[:, j * LANES:(j + 1) * LANES]
            if (n0 // LANES + j) in rope_groups:
                partner = jnp.where(first_half, pltpu.roll(grp, LANES - 16, 1), pltpu.roll(grp, 16, 1))
                grp = grp * cos + partner * sin
            c0 = n0 + j * LANES
            o_ref[:, c0:c0 + LANES] = grp.astype(o_ref.dtype)


def _in_proj_call(x, mod3, mod_row, g, w, cs, *, tm, n_chunk, rope_groups, out_dtype, name):
    b, t, d = x.shape
    n = w.shape[1]
    in_specs = [pl.BlockSpec((None, tm, d), lambda bi, i: (bi, i, 0)),
                pl.BlockSpec((None, 1, mod3.shape[-1]), lambda bi, i: (mod_row(bi), 0, 0)),
                pl.BlockSpec((1, d), lambda bi, i: (0, 0)),
                _resident((d, n), lambda bi, i: (0, 0))]
    args = [x, mod3, g.reshape(1, d), w]
    if rope_groups:
        in_specs.append(pl.BlockSpec((tm, 2 * LANES), lambda bi, i: (i, 0)))
        args.append(cs)
    return pl.pallas_call(
        functools.partial(_in_proj_kernel, rope_groups=rope_groups, n_chunk=n_chunk),
        grid=(b, t // tm),
        in_specs=in_specs,
        out_specs=pl.BlockSpec((None, tm, n), lambda bi, i: (bi, i, 0)),
        out_shape=jax.ShapeDtypeStruct((b, t, n), out_dtype),
        compiler_params=_params("parallel", "parallel"),
        name=name,
    )(*args)


def _diff_attn_kernel(*refs, tk, has_lat, lam_init, n_row_blocks):
    if has_lat:
        lam_ref, q_ref, kc_ref, vc_ref, kl_ref, vl_ref, o_ref, kx, vx, s_a, s_b = refs
    else:
        lam_ref, q_ref, kc_ref, vc_ref, o_ref, kx, vx, s_a, s_b = refs
    lc = kc_ref.shape[0]
    t = kx.shape[0]

    @pl.when(pl.program_id(2) == 0)
    def _():
        kx[0:lc, :] = kc_ref[...]
        vx[0:lc, 0:DA_V_DIM] = vc_ref[...]
        if has_lat:
            kx[lc:t, :] = kl_ref[...]
            vx[lc:t, 0:DA_V_DIM] = vl_ref[...]
        lane_t = lax.broadcasted_iota(jnp.int32, (t, LANES), 1)
        vx[:, DA_V_DIM:DA_V_DIM + LANES] = jnp.where(lane_t == 0, 1.0, 0.0).astype(BF16)

    q = q_ref[...]
    tq = q.shape[0]
    lane = lax.broadcasted_iota(jnp.int32, q.shape, 1)
    zero = jnp.zeros_like(q)
    qmaps = (jnp.where(lane < DA_QK_DIM, q, zero), jnp.where(lane >= DA_QK_DIM, q, zero))
    rows = tq // n_row_blocks
    qm = [qmap[r * rows:(r + 1) * rows] for r in range(n_row_blocks) for qmap in qmaps]

    def chunk(j):
        return pl.ds(j * tk if isinstance(j, int) else pl.multiple_of(j * tk, tk), tk)

    def scores_into(buf, j):
        k = kx[chunk(j), :]
        for ci in range(len(qm)):
            buf[ci] = _dot_nt(qm[ci], k)

    def consume(buf, j, carry):
        v = vx[chunk(j), :]
        out = []
        for ci in range(len(qm)):
            m, acc = carry[ci]
            s = buf[ci]
            m_new = jnp.maximum(m, jnp.max(s, axis=-1, keepdims=True))
            p = jnp.exp2(s - m_new)
            acc = jnp.exp2(m - m_new) * acc + jnp.dot(p.astype(BF16), v, preferred_element_type=F32)
            out.append((m_new, acc))
        return tuple(out)

    n = t // tk
    n_pairs = (n - 1) // 2
    scores_into(s_a, 0)

    def body(i, carry):
        scores_into(s_b, 2 * i + 1)
        carry = consume(s_a, 2 * i, carry)
        scores_into(s_a, 2 * i + 2)
        return consume(s_b, 2 * i + 1, carry)

    carry = tuple((jnp.full((rows, 1), MASK_VALUE, F32), jnp.zeros((rows, DA_V_DIM + LANES), F32)) for _ in qm)
    if n_pairs:
        carry = lax.fori_loop(0, n_pairs, body, carry)
    if n % 2 == 0:
        scores_into(s_b, n - 1)
        carry = consume(s_a, n - 2, carry)
        carry = consume(s_b, n - 1, carry)
    else:
        carry = consume(s_a, n - 1, carry)
    o = [acc[:, 0:DA_V_DIM] / acc[:, DA_V_DIM:DA_V_DIM + 1] for _, acc in carry]
    lv = lam_ref[...]
    lam = (jnp.exp(jnp.sum(lv[0:1] * lv[1:2], axis=-1, keepdims=True))
           - jnp.exp(jnp.sum(lv[2:3] * lv[3:4], axis=-1, keepdims=True)) + lam_init)
    for r in range(n_row_blocks):
        w = o[2 * r] - lam * o[2 * r + 1]
        y = w * lax.rsqrt(jnp.mean(w * w, axis=-1, keepdims=True) + NORM_EPS) * (1.0 - lam_init)
        o_ref[r * rows:(r + 1) * rows, :] = y.astype(o_ref.dtype)


def _diff_attn_call(lam_vec, pq, pc, pl_kv, *, tq, n_row_blocks, tk_cap, lam_init, name):
    b, t, _ = pq.shape
    lc = pc.shape[1]
    kcol, vcol = DA_HEADS, 2 * DA_HEADS
    in_specs = [pl.BlockSpec((4, DA_QK_DIM), lambda bi, h, i: (0, 0)),
                pl.BlockSpec((None, tq, LANES), lambda bi, h, i: (bi, i, h)),
                pl.BlockSpec((None, lc, LANES), lambda bi, h, i: (bi, 0, kcol + h)),
                pl.BlockSpec((None, lc, LANES), lambda bi, h, i: (bi, 0, vcol + h))]
    args = [lam_vec, pq, pc, pc]
    n_keys = lc
    if pl_kv is not None:
        l = pl_kv.shape[1]
        n_keys += l
        in_specs += [pl.BlockSpec((None, l, LANES), lambda bi, h, i: (bi, 0, kcol + h)),
                     pl.BlockSpec((None, l, LANES), lambda bi, h, i: (bi, 0, vcol + h))]
        args += [pl_kv, pl_kv]
    tk = max(c for c in range(LANES, tk_cap + 1, LANES) if n_keys % c == 0)
    return pl.pallas_call(
        functools.partial(_diff_attn_kernel, tk=tk, has_lat=pl_kv is not None, lam_init=lam_init,
                          n_row_blocks=n_row_blocks),
        grid=(b, DA_HEADS, t // tq),
        in_specs=in_specs,
        out_specs=pl.BlockSpec((None, tq, LANES), lambda bi, h, i: (bi, i, h)),
        out_shape=jax.ShapeDtypeStruct((b, t, DA_HEADS * DA_V_DIM), BF16),
        scratch_shapes=[pltpu.VMEM((n_keys, LANES), BF16), pltpu.VMEM((n_keys, DA_V_DIM + LANES), BF16),
                        pltpu.VMEM((2 * n_row_blocks, tq // n_row_blocks, tk), F32),
                        pltpu.VMEM((2 * n_row_blocks, tq // n_row_blocks, tk), F32)],
        compiler_params=_params("arbitrary", "arbitrary", "arbitrary"),
        name=name,
    )(*args)


def _win_attn_kernel(*refs, tq, span, has_win):
    if has_win:
        sink_ref, q_ref, kc_ref, vc_ref, kl_ref, vl_ref, o_ref = refs
    else:
        sink_ref, q_ref, kc_ref, vc_ref, o_ref = refs
    q = q_ref[...]
    kc = kc_ref[...]
    vc = vc_ref[...]
    if has_win:
        l = kl_ref.shape[0]
        q0 = pl.program_id(1) * tq
        start = pl.multiple_of(jnp.clip(q0 - WINDOW, 0, l - span), WINDOW)
        kw = kl_ref[pl.ds(start, span), :]
        vw = vl_ref[pl.ds(start, span), :]
        qpos = q0 + lax.broadcasted_iota(jnp.int32, (tq, span), 0)
        kpos = start + lax.broadcasted_iota(jnp.int32, (tq, span), 1)
        valid = jnp.abs(kpos - qpos) <= WINDOW
    for hk in range(WA_KV_HEADS):
        ks = slice(hk * WA_HEAD_DIM, (hk + 1) * WA_HEAD_DIM)
        kc_h, vc_h = kc[:, ks], vc[:, ks]
        if has_win:
            kw_h, vw_h = kw[:, ks], vw[:, ks]
        for g in range(WA_GROUP):
            h = hk * WA_GROUP + g
            hs = slice(h * WA_HEAD_DIM, (h + 1) * WA_HEAD_DIM)
            q_h = q[:, hs]
            sink = sink_ref[h]
            s_c = _dot_nt(q_h, kc_h)
            m = jnp.maximum(jnp.max(s_c, axis=-1, keepdims=True), sink)
            if has_win:
                s_w = jnp.where(valid, _dot_nt(q_h, kw_h), MASK_VALUE)
                m = jnp.maximum(m, jnp.max(s_w, axis=-1, keepdims=True))
            p_c = jnp.exp(s_c - m)
            den = jnp.sum(p_c, axis=-1, keepdims=True) + jnp.exp(sink - m)
            num = jnp.dot(p_c.astype(BF16), vc_h, preferred_element_type=F32)
            if has_win:
                p_w = jnp.where(valid, jnp.exp(s_w - m), 0.0)
                den = den + jnp.sum(p_w, axis=-1, keepdims=True)
                num = num + jnp.dot(p_w.astype(BF16), vw_h, preferred_element_type=F32)
            o_ref[:, hs] = (num / den).astype(o_ref.dtype)


def _win_attn_call(sink, pq, pc, pl_kv, *, tq, name):
    b, t, _ = pq.shape
    lc = pc.shape[1]
    qcol = 3
    kcol, vcol = 16, 17
    in_specs = [pl.BlockSpec(memory_space=pltpu.SMEM),
                pl.BlockSpec((None, tq, 4 * LANES), lambda bi, i: (bi, i, qcol)),
                pl.BlockSpec((None, lc, LANES), lambda bi, i: (bi, 0, kcol)),
                pl.BlockSpec((None, lc, LANES), lambda bi, i: (bi, 0, vcol))]
    args = [sink, pq, pc, pc]
    has_win = pl_kv is not None
    if has_win:
        l = pl_kv.shape[1]
        in_specs += [pl.BlockSpec((None, l, LANES), lambda bi, i: (bi, 0, kcol)),
                     pl.BlockSpec((None, l, LANES), lambda bi, i: (bi, 0, vcol))]
        args += [pl_kv, pl_kv]
    return pl.pallas_call(
        functools.partial(_win_attn_kernel, tq=tq, span=tq + 2 * WINDOW, has_win=has_win),
        grid=(b, t // tq),
        in_specs=in_specs,
        out_specs=pl.BlockSpec((None, tq, 4 * LANES), lambda bi, i: (bi, i, 0)),
        out_shape=jax.ShapeDtypeStruct((b, t, WA_Q_HEADS * WA_HEAD_DIM), BF16),
        compiler_params=_params("parallel", "arbitrary"),
        name=name,
    )(*args)


def _gla_chunk(q, k, v, g, state, rev):
    c, w = q.shape
    nsub = c // GLA_SUB
    row = lax.broadcasted_iota(jnp.int32, (c, w), 0)
    rb = row % GLA_SUB
    bl = g
    s = 1
    while s < GLA_SUB:
        if rev:
            bl = bl + jnp.where(rb < GLA_SUB - s, pltpu.roll(bl, c - s, 0), 0.0)
        else:
            bl = bl + jnp.where(rb >= s, pltpu.roll(bl, s, 0), 0.0)
        s *= 2
    order = list(range(nsub))[::-1] if rev else list(range(nsub))
    rank = {blk: n for n, blk in enumerate(order)}

    def rows(a, i):
        return a[i * GLA_SUB:(i + 1) * GLA_SUB]

    tot = {}
    for i in range(nsub):
        r = i * GLA_SUB if rev else i * GLA_SUB + GLA_SUB - 1
        tot[i] = bl[r:r + 1]
    pre = {}
    run = jnp.zeros((1, w), F32)
    for i in order:
        pre[i] = run
        run = run + tot[i]
    total = run

    def cat(fn):
        return jnp.concatenate([fn(i) for i in range(nsub)], axis=0)

    zeros = jnp.zeros((GLA_SUB, w), F32)
    b_full = cat(lambda i: rows(bl, i) + pre[i])
    qe = q * jnp.exp(b_full)
    kdec = k * jnp.exp(total - b_full)
    qd = q * jnp.exp(bl)
    kd = k * jnp.exp(-bl)
    ko = cat(lambda i: rows(k, i) * jnp.exp(tot[i] - rows(bl, i)))

    def stack_keys(x):
        r = lax.broadcasted_iota(jnp.int32, (GLA_HEADS * c, w), 0) // c
        ln = lax.broadcasted_iota(jnp.int32, (GLA_HEADS * c, w), 1) // GLA_K_DIM
        return jnp.where(r == ln, jnp.concatenate([x] * GLA_HEADS, axis=0), 0.0).astype(BF16)

    qi = lax.broadcasted_iota(jnp.int32, (c, GLA_HEADS * c), 0)
    kj = lax.broadcasted_iota(jnp.int32, (c, GLA_HEADS * c), 1) % c
    same = (qi // GLA_SUB) == (kj // GLA_SUB)
    tri = (qi <= kj) if rev else (qi >= kj)
    a = jnp.where(same & tri, _dot_nt(qd.astype(BF16), stack_keys(kd)), 0.0)
    for j in order[:-1]:
        off = pre[j] + tot[j]
        qo = cat(lambda i: rows(q, i) * jnp.exp(rows(b_full, i) - off) if rank[i] > rank[j] else zeros)
        koj = cat(lambda i: rows(ko, i) if i == j else zeros)
        a = a + _dot_nt(qo.astype(BF16), stack_keys(koj))

    vr = lax.broadcasted_iota(jnp.int32, (GLA_HEADS * c, v.shape[1]), 0) // c
    vl = lax.broadcasted_iota(jnp.int32, (GLA_HEADS * c, v.shape[1]), 1) // GLA_V_DIM
    vst = jnp.where(vr == vl, jnp.concatenate([v] * GLA_HEADS, axis=0), 0.0).astype(BF16)
    o = _dot_nt(qe.astype(BF16), state.astype(BF16)) + jnp.dot(a.astype(BF16), vst, preferred_element_type=F32)

    sr = lax.broadcasted_iota(jnp.int32, state.shape, 0) // GLA_V_DIM
    sl = lax.broadcasted_iota(jnp.int32, state.shape, 1) // GLA_K_DIM
    upd = _dot_tn(v.astype(BF16), kdec.astype(BF16))
    new_state = state * jnp.exp(total) + jnp.where(sr == sl, upd, 0.0)
    return o, new_state


def _gla_kernel(q_ref, k_ref, v_ref, lr_ref, gw_ref, gb_ref, s0_ref, o_ref, sT_ref, s_scr, *, rev, nc):
    i = pl.program_id(1)

    @pl.when(i == 0)
    def _():
        s_scr[...] = s0_ref[...]

    def body(n, carry):
        r0 = pl.multiple_of(((nc - 1 - n) if rev else n) * GLA_CHUNK, GLA_CHUNK)
        sl = pl.ds(r0, GLA_CHUNK)
        logit = jnp.dot(lr_ref[sl, :], gw_ref[...], preferred_element_type=F32) + gb_ref[...]
        g = (jnp.minimum(logit, 0.0) - jnp.log1p(jnp.exp(-jnp.abs(logit)))) * (1.0 / GLA_GATE_NORM)
        o, new_state = _gla_chunk(q_ref[sl, :], k_ref[sl, :], v_ref[sl, :], g, s_scr[...], rev)
        o_ref[sl, :] = o
        s_scr[...] = new_state
        return carry

    lax.fori_loop(0, nc, body, 0)

    @pl.when(i == pl.num_programs(1) - 1)
    def _():
        sT_ref[...] = s_scr[...]


def _gla_call(p, gw, gb, s0, *, tb, rev, name):
    b, t, _ = p.shape
    nb = t // tb
    hk = GLA_HEADS * GLA_K_DIM
    hv = GLA_HEADS * GLA_V_DIM

    def tblk(i):
        return (nb - 1 - i) if rev else i

    return pl.pallas_call(
        functools.partial(_gla_kernel, rev=rev, nc=tb // GLA_CHUNK),
        grid=(b, nb),
        in_specs=[pl.BlockSpec((None, tb, hk), lambda bi, i: (bi, tblk(i), 0)),
                  pl.BlockSpec((None, tb, hk), lambda bi, i: (bi, tblk(i), 1)),
                  pl.BlockSpec((None, tb, hv), lambda bi, i: (bi, tblk(i), 1)),
                  pl.BlockSpec((None, tb, LANES), lambda bi, i: (bi, tblk(i), 2560 // LANES)),
                  pl.BlockSpec((LANES, hk), lambda bi, i: (0, 0)),
                  pl.BlockSpec((1, hk), lambda bi, i: (0, 0)),
                  pl.BlockSpec((None, hv, hk), lambda bi, i: (bi, 0, 0))],
        out_specs=[pl.BlockSpec((None, tb, hv), lambda bi, i: (bi, tblk(i), 0)),
                   pl.BlockSpec((None, hv, hk), lambda bi, i: (bi, 0, 0))],
        out_shape=[jax.ShapeDtypeStruct((b, t, hv), F32), jax.ShapeDtypeStruct((b, hv, hk), F32)],
        scratch_shapes=[pltpu.VMEM((hv, hk), F32)],
        compiler_params=_params("parallel", "arbitrary"),
        name=name,
    )(p, p, p, p, gw, gb, s0)


def _lru_kernel(zx_ref, zp_ref, zn_ref, cw_ref, cb_ref, wg_ref, bg_ref, lam_ref, h0_ref,
                h_ref, hT_ref, carry_scr, *, rev, nb):
    i = pl.program_id(1)
    blk = (nb - 1 - i) if rev else i

    @pl.when(i == 0)
    def _():
        carry_scr[...] = h0_ref[...]

    x = zx_ref[...]
    tb, w = x.shape
    row = lax.broadcasted_iota(jnp.int32, (tb, w), 0)
    pm = jnp.where(blk > 0, zp_ref[SUBLANES - 1:SUBLANES, :], 0.0)
    n0 = jnp.where(blk < nb - 1, zn_ref[0:1, :], 0.0)
    n1 = jnp.where(blk < nb - 1, zn_ref[1:2, :], 0.0)
    xm1 = jnp.where(row == 0, pm, pltpu.roll(x, 1, 0))
    xp1 = jnp.where(row == tb - 1, n0, pltpu.roll(x, tb - 1, 0))
    xp2 = jnp.where(row == tb - 1, n1, jnp.where(row == tb - 2, n0, pltpu.roll(x, tb - 2, 0)))
    cw = cw_ref[...]
    xr = xm1 * cw[0:1] + x * cw[1:2] + xp1 * cw[2:3] + xp2 * cw[3:4] + cb_ref[...]

    gm = jnp.dot(xr.astype(BF16), wg_ref[...], preferred_element_type=F32) + bg_ref[...]
    r = jax.nn.sigmoid(gm[:, 0:w])
    ig = jax.nn.sigmoid(gm[:, w:2 * w])
    log_a = (-LRU_C * r) * _softplus(-lam_ref[...])
    a = jnp.exp(log_a)
    u = jnp.sqrt(-jnp.tanh(log_a) * (a * a + 1.0)) * (ig * xr)

    s = 1
    while s < tb:
        if rev:
            keep = row < tb - s
            a_sh = jnp.where(keep, pltpu.roll(a, tb - s, 0), 1.0)
            u_sh = jnp.where(keep, pltpu.roll(u, tb - s, 0), 0.0)
        else:
            keep = row >= s
            a_sh = jnp.where(keep, pltpu.roll(a, s, 0), 1.0)
            u_sh = jnp.where(keep, pltpu.roll(u, s, 0), 0.0)
        u = a * u_sh + u
        a = a * a_sh
        s *= 2
    h = u + a * carry_scr[0:1, :]
    h_ref[...] = h
    last = h[0:1] if rev else h[tb - 1:tb]
    carry_scr[...] = jnp.broadcast_to(last, carry_scr.shape)

    @pl.when(i == nb - 1)
    def _():
        hT_ref[...] = carry_scr[...]


def _lru_call(p, cw, cb, wg, bg, lam, h0, *, tb, rev, name):
    b, t, _ = p.shape
    nb = t // tb
    w = LRU_WIDTH
    zcol = 2048 // w
    per = tb // SUBLANES

    def tblk(i):
        return (nb - 1 - i) if rev else i

    return pl.pallas_call(
        functools.partial(_lru_kernel, rev=rev, nb=nb),
        grid=(b, nb),
        in_specs=[pl.BlockSpec((None, tb, w), lambda bi, i: (bi, tblk(i), zcol)),
                  pl.BlockSpec((None, SUBLANES, w), lambda bi, i: (bi, jnp.maximum(tblk(i) * per - 1, 0), zcol)),
                  pl.BlockSpec((None, SUBLANES, w),
                               lambda bi, i: (bi, jnp.minimum((tblk(i) + 1) * per, nb * per - 1), zcol)),
                  pl.BlockSpec((LRU_CONV, w), lambda bi, i: (0, 0)),
                  pl.BlockSpec((1, w), lambda bi, i: (0, 0)),
                  pl.BlockSpec((w, 2 * w), lambda bi, i: (0, 0)),
                  pl.BlockSpec((1, 2 * w), lambda bi, i: (0, 0)),
                  pl.BlockSpec((1, w), lambda bi, i: (0, 0)),
                  pl.BlockSpec((None, SUBLANES, w), lambda bi, i: (bi, 0, 0))],
        out_specs=[pl.BlockSpec((None, tb, w), lambda bi, i: (bi, tblk(i), 0)),
                   pl.BlockSpec((None, SUBLANES, w), lambda bi, i: (bi, 0, 0))],
        out_shape=[jax.ShapeDtypeStruct((b, t, w), F32), jax.ShapeDtypeStruct((b, SUBLANES, w), F32)],
        scratch_shapes=[pltpu.VMEM((SUBLANES, w), F32)],
        compiler_params=_params("parallel", "arbitrary"),
        name=name,
    )(p, p, p, cw, cb, wg, bg, lam, h0)


def _out_ffn_kernel(*refs, kind, final, f_chunk):
    if kind == 0:
        x_ref, ma_ref, mb_ref = refs[0:3]
        rest = refs[3:]
    else:
        x_ref, of_ref, ob_ref, hf_ref, hb_ref, gg_ref, zg_ref, glag_ref = refs[0:8]
        rest = refs[8:]
    if final:
        mod_ref, g2_ref, wo_ref, wi_ref, wf_ref, fg_ref, o_ref = rest
    else:
        mod_ref, g2_ref, wo_ref, wi_ref, wf_ref, o_ref = rest
    x = x_ref[...]
    d = x.shape[-1]
    if kind == 0:
        mix_a = ma_ref[...]
        mix_b = mb_ref[...]
    else:
        og = of_ref[...] + ob_ref[...]
        gg = gg_ref[...]
        parts = []
        for h in range(GLA_HEADS):
            hs = slice(h * GLA_V_DIM, (h + 1) * GLA_V_DIM)
            seg = og[:, hs]
            nrm = seg * lax.rsqrt(jnp.mean(seg * seg, axis=-1, keepdims=True) + NORM_EPS) * glag_ref[...]
            parts.append(nrm * _silu(gg[:, hs]))
        mix_a = jnp.concatenate(parts, axis=-1).astype(BF16)
        mix_b = ((hf_ref[...] + hb_ref[...]) * _gelu_tanh(zg_ref[...])).astype(BF16)
    wa = mix_a.shape[-1]
    y = (jnp.dot(mix_a, wo_ref[0:wa, :], preferred_element_type=F32)
         + jnp.dot(mix_b, wo_ref[wa:, :], preferred_element_type=F32))
    mod = mod_ref[...]
    x1 = x + mod[:, 2 * d:3 * d] * y
    h2 = _norm_mod(x1, g2_ref[...], mod[:, 3 * d:4 * d], mod[:, 4 * d:5 * d]).astype(BF16)
    f = wf_ref.shape[0]
    acc = None
    for f0 in range(0, f, f_chunk):
        gate = jnp.dot(h2, wi_ref[:, f0:f0 + f_chunk], preferred_element_type=F32)
        up = jnp.dot(h2, wi_ref[:, f + f0:f + f0 + f_chunk], preferred_element_type=F32)
        act = (_silu(gate) * up).astype(BF16)
        part = jnp.dot(act, wf_ref[f0:f0 + f_chunk, :], preferred_element_type=F32)
        acc = part if acc is None else acc + part
    x2 = x1 + mod[:, 5 * d:6 * d] * acc
    if final:
        ms = jnp.mean(x2 * x2, axis=-1, keepdims=True)
        x2 = x2 * lax.rsqrt(ms + NORM_EPS) * fg_ref[...]
    o_ref[...] = x2


def _out_ffn_call(x, mix, mod3, mod_row, g2, wo, wi, wf, final_g, *, kind, tm, f_chunk, name):
    b, t, d = x.shape
    f = wf.shape[0]

    def tile(width, col=0):
        return pl.BlockSpec((None, tm, width), lambda bi, i: (bi, i, col))

    in_specs = [tile(d)]
    args = [x]
    if kind == 0:
        ma, mb = mix
        in_specs += [tile(ma.shape[-1]), tile(mb.shape[-1])]
        args += [ma, mb]
    else:
        o_f, o_b, h_f, h_b, p, gla_g = mix
        hv = GLA_HEADS * GLA_V_DIM
        in_specs += [tile(hv), tile(hv), tile(LRU_WIDTH), tile(LRU_WIDTH),
                     tile(hv, 1024 // hv), tile(LRU_WIDTH, 1536 // LRU_WIDTH),
                     pl.BlockSpec((1, GLA_V_DIM), lambda bi, i: (0, 0))]
        args += [o_f, o_b, h_f, h_b, p, p, gla_g.reshape(1, GLA_V_DIM)]
    in_specs += [pl.BlockSpec((None, 1, mod3.shape[-1]), lambda bi, i: (mod_row(bi), 0, 0)),
                 pl.BlockSpec((1, d), lambda bi, i: (0, 0)),
                 _resident(wo.shape, lambda bi, i: (0, 0)),
                 _resident(wi.shape, lambda bi, i: (0, 0)),
                 _resident(wf.shape, lambda bi, i: (0, 0))]
    args += [mod3, g2.reshape(1, d), wo, wi, wf]
    final = final_g is not None
    if final:
        in_specs.append(pl.BlockSpec((1, d), lambda bi, i: (0, 0)))
        args.append(final_g.reshape(1, d))
    return pl.pallas_call(
        functools.partial(_out_ffn_kernel, kind=kind, final=final, f_chunk=f_chunk),
        grid=(b, t // tm),
        in_specs=in_specs,
        out_specs=pl.BlockSpec((None, tm, d), lambda bi, i: (bi, i, 0)),
        out_shape=jax.ShapeDtypeStruct((b, t, d), F32),
        compiler_params=_params("parallel", "parallel"),
        name=name,
    )(*args)


def _rope_table(n_tokens):
    n_rows = n_tokens // GRID_W
    row = jnp.broadcast_to(jnp.arange(n_rows)[:, None], (n_rows, GRID_W)).reshape(-1)
    col = jnp.broadcast_to(jnp.arange(GRID_W)[None, :], (n_rows, GRID_W)).reshape(-1)
    axis_dim = ROPE_DIM // 2
    inv_freq = ROPE_THETA ** (-jnp.arange(0, axis_dim, 2, dtype=F32) / axis_dim)
    ang_r = row.astype(F32)[:, None] * inv_freq
    ang_c = col.astype(F32)[:, None] * inv_freq
    cr, sr, cc, sc = jnp.cos(ang_r), jnp.sin(ang_r), jnp.cos(ang_c), jnp.sin(ang_c)
    cos = jnp.concatenate([cr, cr, cc, cc], axis=-1)
    sin = jnp.concatenate([-sr, sr, -sc, sc], axis=-1)
    reps = LANES // ROPE_DIM
    return jnp.concatenate([jnp.tile(cos, (1, reps)), jnp.tile(sin, (1, reps))], axis=-1)


def _even_w_in(w):
    qa, ka, va, qb, kb, vb = jnp.split(w, (512, 1024, 1536, 2048, 2176), axis=-1)
    qa_scale = DA_QK_DIM ** -0.5 * math.log2(math.e)
    return jnp.concatenate([qa * qa_scale, ka, va, qb * WA_HEAD_DIM ** -0.5, kb, vb], axis=-1).astype(BF16)


def _odd_w_in(w):
    q, k, v, g, lr, zg, zx = jnp.split(w, (256, 512, 1024, 1536, 1568, 2080), axis=-1)
    pad = jnp.zeros((w.shape[0], LANES - lr.shape[1]), w.dtype)
    return jnp.concatenate([q * GLA_K_DIM ** -0.5, k, v, g, zg, zx, lr, pad], axis=-1).astype(BF16)


def _block_diag(w):
    n, c, d = w.shape
    eye = jnp.eye(n, dtype=w.dtype)
    return (eye[:, None, :, None] * w[:, :, None, :]).reshape(n * c, n * d)


def _tile_rows(n, cap):
    t = min(n, cap)
    while n % t:
        t //= 2
    return t


def kernel(x, c, ctx, c_ctx, ada_w, ada_b, norm_g, even_w_in, even_w_out, diff_lam, win_sink, odd_w_in, odd_w_out, gla_gate_w, gla_gate_b, gla_norm_g, lru_conv_w, lru_conv_b, lru_wa, lru_ba, lru_wx, lru_bx, lru_lam, ffn_w_in, ffn_w_out, final_g):
    b, l, d = x.shape
    lc = ctx.shape[1]
    depth = ada_w.shape[0]
    assert b < SUBLANES and d == 1024

    cond = jnp.zeros((SUBLANES, d), F32).at[0:b].set(c).at[b].set(c_ctx)
    mod = _ada_call(cond, ada_w, ada_b)
    cs = _rope_table(l)

    xl = x
    xc = ctx.reshape(1, b * lc, d)
    tm_l = _tile_rows(l, 512)
    tm_c = _tile_rows(b * lc, 512)
    lat_row = lambda bi: bi
    ctx_row = lambda bi: b

    for li in range(depth):
        last = li == depth - 1
        mod3 = mod[li].reshape(SUBLANES, 1, 6 * d)
        wf_in = ffn_w_in[li].astype(BF16)
        wf_out = ffn_w_out[li].astype(BF16)
        fg = final_g if last else None
        f_chunk = wf_out.shape[0] // 2
        if li % 2 == 0:
            e = li // 2
            lam_init = 0.8 - 0.6 * math.exp(-0.3 * li)
            w_in = _even_w_in(even_w_in[e])
            w_out = even_w_out[e].astype(BF16)
            rope_groups = frozenset(range(0, 8)) | frozenset(range(12, 17))
            p_l = _in_proj_call(xl, mod3, lat_row, norm_g[li, 0], w_in, cs, tm=tm_l, n_chunk=768,
                                rope_groups=rope_groups, out_dtype=BF16, name=f"in_proj_l{li}")
            p_c = _in_proj_call(xc, mod3, ctx_row, norm_g[li, 0], w_in, None, tm=tm_c, n_chunk=768,
                                rope_groups=None, out_dtype=BF16, name=f"in_proj_c{li}")
            p_c = p_c.reshape(b, lc, EVEN_IN)
            tq = _tile_rows(l, 256)
            a_l = _diff_attn_call(diff_lam[e], p_l, p_c, p_l, tq=_tile_rows(l, 512), n_row_blocks=2, tk_cap=768,
                                  lam_init=lam_init, name=f"diff_attn_l{li}")
            b_l = _win_attn_call(win_sink[e], p_l, p_c, p_l, tq=tq, name=f"win_attn_l{li}")
            xl = _out_ffn_call(xl, (a_l, b_l), mod3, lat_row, norm_g[li, 1], w_out, wf_in, wf_out, fg,
                               kind=0, tm=tm_l, f_chunk=f_chunk, name=f"out_ffn_l{li}")
            if not last:
                a_c = _diff_attn_call(diff_lam[e], p_c, p_c, None, tq=lc, n_row_blocks=2, tk_cap=768,
                                      lam_init=lam_init, name=f"diff_attn_c{li}")
                b_c = _win_attn_call(win_sink[e], p_c, p_c, None, tq=lc, name=f"win_attn_c{li}")
                mix_c = (a_c.reshape(1, b * lc, -1), b_c.reshape(1, b * lc, -1))
                xc = _out_ffn_call(xc, mix_c, mod3, ctx_row, norm_g[li, 1], w_out, wf_in, wf_out, None,
                                   kind=0, tm=tm_c, f_chunk=f_chunk, name=f"out_ffn_c{li}")
        else:
            o = li // 2
            w_in = _odd_w_in(odd_w_in[o])
            w_out = odd_w_out[o].astype(BF16)
            p_l = _in_proj_call(xl, mod3, lat_row, norm_g[li, 0], w_in, None, tm=tm_l, n_chunk=896,
                                rope_groups=None, out_dtype=F32, name=f"in_proj_l{li}")
            p_c = _in_proj_call(xc, mod3, ctx_row, norm_g[li, 0], w_in, None, tm=tm_c, n_chunk=896,
                                rope_groups=None, out_dtype=F32, name=f"in_proj_c{li}")
            p_c = p_c.reshape(b, lc, ODD_IN_PAD)
            hk = GLA_HEADS * GLA_K_DIM
            hv = GLA_HEADS * GLA_V_DIM
            s0 = jnp.zeros((b, hv, hk), F32)
            h0 = jnp.zeros((b, SUBLANES, LRU_WIDTH), F32)
            cw = lru_conv_w[o]
            cb = lru_conv_b[o].reshape(1, LRU_WIDTH)
            outs = []
            for dr in range(2):
                rev = dr == 1
                gw = jnp.zeros((LANES, hk), F32).at[dr * GLA_GATE_RANK:(dr + 1) * GLA_GATE_RANK].set(gla_gate_w[o, dr])
                gb = gla_gate_b[o, dr].reshape(1, hk)
                o_c, s_c = _gla_call(p_c, gw, gb, s0, tb=_tile_rows(lc, 256), rev=rev, name=f"gla_c{li}d{dr}")
                o_l, _ = _gla_call(p_l, gw, gb, s_c, tb=_tile_rows(l, 512), rev=rev, name=f"gla_l{li}d{dr}")
                wg = jnp.concatenate([_block_diag(lru_wa[o, dr]), _block_diag(lru_wx[o, dr])], axis=-1).astype(BF16)
                bg = jnp.concatenate([lru_ba[o, dr], lru_bx[o, dr]]).reshape(1, 2 * LRU_WIDTH)
                lam = lru_lam[o, dr].reshape(1, LRU_WIDTH)
                h_c, hT = _lru_call(p_c, cw, cb, wg, bg, lam, h0, tb=_tile_rows(lc, 256), rev=rev,
                                    name=f"lru_c{li}d{dr}")
                h_l, _ = _lru_call(p_l, cw, cb, wg, bg, lam, hT, tb=_tile_rows(l, 256), rev=rev,
                                   name=f"lru_l{li}d{dr}")
                outs.append((o_c, o_l, h_c, h_l))
            mix_l = (outs[0][1], outs[1][1], outs[0][3], outs[1][3], p_l, gla_norm_g[o])
            xl = _out_ffn_call(xl, mix_l, mod3, lat_row, norm_g[li, 1], w_out, wf_in, wf_out, fg,
                               kind=1, tm=tm_l, f_chunk=f_chunk, name=f"out_ffn_l{li}")
            if not last:
                flat = lambda a: a.reshape(1, b * lc, a.shape[-1])
                mix_c = (flat(outs[0][0]), flat(outs[1][0]), flat(outs[0][2]), flat(outs[1][2]),
                         flat(p_c), gla_norm_g[o])
                xc = _out_ffn_call(xc, mix_c, mod3, ctx_row, norm_g[li, 1], w_out, wf_in, wf_out, None,
                                   kind=1, tm=tm_c, f_chunk=f_chunk, name=f"out_ffn_c{li}")
    return xl
```

```python
import functools
import math

import jax
import jax.numpy as jnp
from jax import lax
from jax.experimental import pallas as pl
from jax.experimental.pallas import tpu as pltpu

F32 = jnp.float32
BF16 = jnp.bfloat16

NORM_EPS = 1e-6
ROPE_THETA = 10000.0
ROPE_DIM = 64
GRID_W = 64
MASK_VALUE = -1e30

DA_HEADS = 4
DA_QK_DIM = 64
DA_V_DIM = 128
WA_Q_HEADS = 8
WA_KV_HEADS = 2
WA_GROUP = WA_Q_HEADS // WA_KV_HEADS
WA_HEAD_DIM = 64
WINDOW = 128

GLA_HEADS = 4
GLA_K_DIM = 64
GLA_V_DIM = 128
GLA_GATE_RANK = 16
GLA_GATE_NORM = 16.0
LRU_WIDTH = 512
LRU_BLOCKS = 8
LRU_CONV = 4
LRU_C = 8.0

LANES = 128
SUBLANES = 8
VMEM_LIMIT_BYTES = 56 * 1024 * 1024

GLA_CHUNK = 64
GLA_SUB = 16
EVEN_IN = 2304
ODD_IN_PAD = 2688


def _params(*sem):
    return pltpu.CompilerParams(dimension_semantics=sem, vmem_limit_bytes=VMEM_LIMIT_BYTES)


def _resident(block_shape, index_map):
    return pl.BlockSpec(block_shape, index_map, pipeline_mode=pl.Buffered(1))


def _softplus(x):
    return jnp.maximum(x, 0.0) + jnp.log1p(jnp.exp(-jnp.abs(x)))


def _silu(x):
    return x * jax.nn.sigmoid(x)


def _gelu_tanh(x):
    return 0.5 * x * (1.0 + jnp.tanh(math.sqrt(2.0 / math.pi) * (x + 0.044715 * (x * x * x))))


def _dot_nt(a, b):
    return lax.dot_general(a, b, (((1,), (1,)), ((), ())), preferred_element_type=F32)


def _dot_tn(a, b):
    return lax.dot_general(a, b, (((0,), (0,)), ((), ())), preferred_element_type=F32)


def _ada_kernel(cond_ref, w_ref, b_ref, o_ref):
    s = _silu(cond_ref[...])
    o_ref[...] = jnp.dot(s, w_ref[...], preferred_element_type=F32,
                         precision=lax.Precision.HIGHEST) + b_ref[...]


def _ada_call(cond, ada_w, ada_b):
    depth, d, n = ada_w.shape
    tn = 1536
    return pl.pallas_call(
        _ada_kernel,
        grid=(depth, n // tn),
        in_specs=[pl.BlockSpec((SUBLANES, d), lambda l, j: (0, 0)),
                  pl.BlockSpec((None, d, tn), lambda l, j: (l, 0, j)),
                  pl.BlockSpec((None, 1, tn), lambda l, j: (l, 0, j))],
        out_specs=pl.BlockSpec((None, SUBLANES, tn), lambda l, j: (l, 0, j)),
        out_shape=jax.ShapeDtypeStruct((depth, SUBLANES, n), F32),
        compiler_params=_params("arbitrary", "arbitrary"),
        name="ada_mod",
    )(cond, ada_w, ada_b.reshape(depth, 1, n))


def _norm_mod(x, g, shift, scale):
    ms = jnp.mean(x * x, axis=-1, keepdims=True)
    return (x * lax.rsqrt(ms + NORM_EPS) * g) * (1.0 + scale) + shift


def _in_proj_kernel(*refs, rope_groups, n_chunk):
    if rope_groups:
        x_ref, mod_ref, g_ref, w_ref, cs_ref, o_ref = refs
    else:
        x_ref, mod_ref, g_ref, w_ref, o_ref = refs
    x = x_ref[...]
    d = x.shape[-1]
    mod = mod_ref[...]
    hb = _norm_mod(x, g_ref[...], mod[:, 0:d], mod[:, d:2 * d]).astype(BF16)
    n = w_ref.shape[1]
    if rope_groups:
        cos = cs_ref[:, 0:LANES]
        sin = cs_ref[:, LANES:2 * LANES]
        lane = lax.broadcasted_iota(jnp.int32, (x.shape[0], LANES), 1)
        first_half = (lane % 32) < 16
    for n0 in range(0, n, n_chunk):
        acc = jnp.dot(hb, w_ref[:, n0:n0 + n_chunk], preferred_element_type=F32)
        if not rope_groups:
            o_ref[:, n0:n0 + n_chunk] = acc.astype(o_ref.dtype)
            continue
        for j in range(n_chunk // LANES):
            grp = acc[:, j * LANES:(j + 1) * LANES]
            if (n0 // LANES + j) in rope_groups:
                partner = jnp.where(first_half, pltpu.roll(grp, LANES - 16, 1), pltpu.roll(grp, 16, 1))
                grp = grp * cos + partner * sin
            c0 = n0 + j * LANES
            o_ref[:, c0:c0 + LANES] = grp.astype(o_ref.dtype)


def _in_proj_call(x, mod3, mod_row, g, w, cs, *, tm, n_chunk, rope_groups, out_dtype, name):
    b, t, d = x.shape
    n = w.shape[1]
    in_specs = [pl.BlockSpec((None, tm, d), lambda bi, i: (bi, i, 0)),
                pl.BlockSpec((None, 1, mod3.shape[-1]), lambda bi, i: (mod_row(bi), 0, 0)),
                pl.BlockSpec((1, d), lambda bi, i: (0, 0)),
                _resident((d, n), lambda bi, i: (0, 0))]
    args = [x, mod3, g.reshape(1, d), w]
    if rope_groups:
        in_specs.append(pl.BlockSpec((tm, 2 * LANES), lambda bi, i: (i, 0)))
        args.append(cs)
    return pl.pallas_call(
        functools.partial(_in_proj_kernel, rope_groups=rope_groups, n_chunk=n_chunk),
        grid=(b, t // tm),
        in_specs=in_specs,
        out_specs=pl.BlockSpec((None, tm, n), lambda bi, i: (bi, i, 0)),
        out_shape=jax.ShapeDtypeStruct((b, t, n), out_dtype),
        compiler_params=_params("parallel", "parallel"),
        name=name,
    )(*args)


def _diff_attn_kernel(*refs, tk, has_lat, lam_init, n_row_blocks):
    if has_lat:
        lam_ref, q_ref, kc_ref, vc_ref, kl_ref, vl_ref, o_ref, kx, vx, s_a, s_b = refs
    else:
        lam_ref, q_ref, kc_ref, vc_ref, o_ref, kx, vx, s_a, s_b = refs
    lc = kc_ref.shape[0]
    t = kx.shape[0]

    @pl.when(pl.program_id(2) == 0)
    def _():
        kx[0:lc, :] = kc_ref[...]
        vx[0:lc, 0:DA_V_DIM] = vc_ref[...]
        if has_lat:
            kx[lc:t, :] = kl_ref[...]
            vx[lc:t, 0:DA_V_DIM] = vl_ref[...]
        lane_t = lax.broadcasted_iota(jnp.int32, (t, LANES), 1)
        vx[:, DA_V_DIM:DA_V_DIM + LANES] = jnp.where(lane_t == 0, 1.0, 0.0).astype(BF16)

    q = q_ref[...]
    tq = q.shape[0]
    lane = lax.broadcasted_iota(jnp.int32, q.shape, 1)
    zero = jnp.zeros_like(q)
    qmaps = (jnp.where(lane < DA_QK_DIM, q, zero), jnp.where(lane >= DA_QK_DIM, q, zero))
    rows = tq // n_row_blocks
    qm = [qmap[r * rows:(r + 1) * rows] for r in range(n_row_blocks) for qmap in qmaps]

    def chunk(j):
        return pl.ds(j * tk if isinstance(j, int) else pl.multiple_of(j * tk, tk), tk)

    def scores_into(buf, j):
        k = kx[chunk(j), :]
        for ci in range(len(qm)):
            buf[ci] = _dot_nt(qm[ci], k)

    def consume(buf, j, carry):
        v = vx[chunk(j), :]
        out = []
        for ci in range(len(qm)):
            m, acc = carry[ci]
            s = buf[ci]
            m_new = jnp.maximum(m, jnp.max(s, axis=-1, keepdims=True))
            p = jnp.exp2(s - m_new)
            acc = jnp.exp2(m - m_new) * acc + jnp.dot(p.astype(BF16), v, preferred_element_type=F32)
            out.append((m_new, acc))
        return tuple(out)

    n = t // tk
    n_pairs = (n - 1) // 2
    buf_a, buf_b = s_a, s_b
    scores_into(buf_a, 0)

    def body(i, carry):
        scores_into(buf_b, 2 * i + 1)
        carry = consume(buf_a, 2 * i, carry)
        scores_into(buf_a, 2 * i + 2)
        return consume(buf_b, 2 * i + 1, carry)

    carry = tuple((jnp.full((rows, 1), MASK_VALUE, F32), jnp.zeros((rows, DA_V_DIM + LANES), F32)) for _ in qm)
    for i in range(n_pairs):
        carry = body(i, carry)
    if n % 2 == 0:
        scores_into(buf_b, n - 1)
        carry = consume(buf_a, n - 2, carry)
        carry = consume(buf_b, n - 1, carry)
    else:
        carry = consume(buf_a, n - 1, carry)
    o = [acc[:, 0:DA_V_DIM] / acc[:, DA_V_DIM:DA_V_DIM + 1] for _, acc in carry]
    lv = lam_ref[...]
    lam = (jnp.exp(jnp.sum(lv[0:1] * lv[1:2], axis=-1, keepdims=True))
           - jnp.exp(jnp.sum(lv[2:3] * lv[3:4], axis=-1, keepdims=True)) + lam_init)
    for r in range(n_row_blocks):
        w = o[2 * r] - lam * o[2 * r + 1]
        y = w * lax.rsqrt(jnp.mean(w * w, axis=-1, keepdims=True) + NORM_EPS) * (1.0 - lam_init)
        o_ref[r * rows:(r + 1) * rows, :] = y.astype(o_ref.dtype)


def _diff_attn_call(lam_vec, pq, pc, pl_kv, *, tq, n_row_blocks, tk_cap, lam_init, name):
    b, t, _ = pq.shape
    lc = pc.shape[1]
    kcol, vcol = DA_HEADS, 2 * DA_HEADS
    in_specs = [pl.BlockSpec((4, DA_QK_DIM), lambda bi, h, i: (0, 0)),
                pl.BlockSpec((None, tq, LANES), lambda bi, h, i: (bi, i, h)),
                pl.BlockSpec((None, lc, LANES), lambda bi, h, i: (bi, 0, kcol + h)),
                pl.BlockSpec((None, lc, LANES), lambda bi, h, i: (bi, 0, vcol + h))]
    args = [lam_vec, pq, pc, pc]
    n_keys = lc
    if pl_kv is not None:
        l = pl_kv.shape[1]
        n_keys += l
        in_specs += [pl.BlockSpec((None, l, LANES), lambda bi, h, i: (bi, 0, kcol + h)),
                     pl.BlockSpec((None, l, LANES), lambda bi, h, i: (bi, 0, vcol + h))]
        args += [pl_kv, pl_kv]
    tk = max(c for c in range(LANES, tk_cap + 1, LANES) if n_keys % c == 0)
    return pl.pallas_call(
        functools.partial(_diff_attn_kernel, tk=tk, has_lat=pl_kv is not None, lam_init=lam_init,
                          n_row_blocks=n_row_blocks),
        grid=(b, DA_HEADS, t // tq),
        in_specs=in_specs,
        out_specs=pl.BlockSpec((None, tq, LANES), lambda bi, h, i: (bi, i, h)),
        out_shape=jax.ShapeDtypeStruct((b, t, DA_HEADS * DA_V_DIM), BF16),
        scratch_shapes=[pltpu.VMEM((n_keys, LANES), BF16), pltpu.VMEM((n_keys, DA_V_DIM + LANES), BF16),
                        pltpu.VMEM((2 * n_row_blocks, tq // n_row_blocks, tk), F32),
                        pltpu.VMEM((2 * n_row_blocks, tq // n_row_blocks, tk), F32)],
        compiler_params=_params("arbitrary", "arbitrary", "arbitrary"),
        name=name,
    )(*args)


def _win_attn_kernel(*refs, tq, span, has_win):
    if has_win:
        sink_ref, q_ref, kc_ref, vc_ref, kl_ref, vl_ref, o_ref = refs
    else:
        sink_ref, q_ref, kc_ref, vc_ref, o_ref = refs
    q = q_ref[...]
    kc = kc_ref[...]
    vc = vc_ref[...]
    if has_win:
        l = kl_ref.shape[0]
        q0 = pl.program_id(1) * tq
        start = pl.multiple_of(jnp.clip(q0 - WINDOW, 0, l - span), WINDOW)
        kw = kl_ref[pl.ds(start, span), :]
        vw = vl_ref[pl.ds(start, span), :]
        qpos = q0 + lax.broadcasted_iota(jnp.int32, (tq, span), 0)
        kpos = start + lax.broadcasted_iota(jnp.int32, (tq, span), 1)
        valid = jnp.abs(kpos - qpos) <= WINDOW
    for hk in range(WA_KV_HEADS):
        ks = slice(hk * WA_HEAD_DIM, (hk + 1) * WA_HEAD_DIM)
        kc_h, vc_h = kc[:, ks], vc[:, ks]
        if has_win:
            kw_h, vw_h = kw[:, ks], vw[:, ks]
        for g in range(WA_GROUP):
            h = hk * WA_GROUP + g
            hs = slice(h * WA_HEAD_DIM, (h + 1) * WA_HEAD_DIM)
            q_h = q[:, hs]
            sink = sink_ref[h] * math.log2(math.e)
            s_c = _dot_nt(q_h, kc_h)
            m = jnp.maximum(jnp.max(s_c, axis=-1, keepdims=True), sink)
            if has_win:
                s_w = jnp.where(valid, _dot_nt(q_h, kw_h), MASK_VALUE)
                m = jnp.maximum(m, jnp.max(s_w, axis=-1, keepdims=True))
            p_c = jnp.exp2(s_c - m)
            den = jnp.sum(p_c, axis=-1, keepdims=True) + jnp.exp2(sink - m)
            num = jnp.dot(p_c.astype(BF16), vc_h, preferred_element_type=F32)
            if has_win:
                p_w = jnp.exp2(s_w - m)
                den = den + jnp.sum(p_w, axis=-1, keepdims=True)
                num = num + jnp.dot(p_w.astype(BF16), vw_h, preferred_element_type=F32)
            o_ref[:, hs] = (num / den).astype(o_ref.dtype)


def _win_attn_call(sink, pq, pc, pl_kv, *, tq, name):
    b, t, _ = pq.shape
    lc = pc.shape[1]
    qcol = 3
    kcol, vcol = 16, 17
    in_specs = [pl.BlockSpec(memory_space=pltpu.SMEM),
                pl.BlockSpec((None, tq, 4 * LANES), lambda bi, i: (bi, i, qcol)),
                pl.BlockSpec((None, lc, LANES), lambda bi, i: (bi, 0, kcol)),
                pl.BlockSpec((None, lc, LANES), lambda bi, i: (bi, 0, vcol))]
    args = [sink, pq, pc, pc]
    has_win = pl_kv is not None
    if has_win:
        l = pl_kv.shape[1]
        in_specs += [pl.BlockSpec((None, l, LANES), lambda bi, i: (bi, 0, kcol)),
                     pl.BlockSpec((None, l, LANES), lambda bi, i: (bi, 0, vcol))]
        args += [pl_kv, pl_kv]
    return pl.pallas_call(
        functools.partial(_win_attn_kernel, tq=tq, span=tq + 2 * WINDOW, has_win=has_win),
        grid=(b, t // tq),
        in_specs=in_specs,
        out_specs=pl.BlockSpec((None, tq, 4 * LANES), lambda bi, i: (bi, i, 0)),
        out_shape=jax.ShapeDtypeStruct((b, t, WA_Q_HEADS * WA_HEAD_DIM), BF16),
        compiler_params=_params("parallel", "arbitrary"),
        name=name,
    )(*args)


def _gla_chunk(q, k, v, g, state, rev):
    c, w = q.shape
    nsub = c // GLA_SUB
    row = lax.broadcasted_iota(jnp.int32, (c, w), 0)
    rb = row % GLA_SUB
    bl = g
    s = 1
    while s < GLA_SUB:
        if rev:
            bl = bl + jnp.where(rb < GLA_SUB - s, pltpu.roll(bl, c - s, 0), 0.0)
        else:
            bl = bl + jnp.where(rb >= s, pltpu.roll(bl, s, 0), 0.0)
        s *= 2
    order = list(range(nsub))[::-1] if rev else list(range(nsub))
    rank = {blk: n for n, blk in enumerate(order)}

    def rows(a, i):
        return a[i * GLA_SUB:(i + 1) * GLA_SUB]

    tot = {}
    for i in range(nsub):
        r = i * GLA_SUB if rev else i * GLA_SUB + GLA_SUB - 1
        tot[i] = bl[r:r + 1]
    pre = {}
    run = jnp.zeros((1, w), F32)
    for i in order:
        pre[i] = run
        run = run + tot[i]
    total = run

    def cat(fn):
        return jnp.concatenate([fn(i) for i in range(nsub)], axis=0)

    zeros = jnp.zeros((GLA_SUB, w), F32)
    b_full = cat(lambda i: rows(bl, i) + pre[i])
    qe = q * jnp.exp(b_full)
    kdec = k * jnp.exp(total - b_full)
    qd = q * jnp.exp(bl)
    kd = k * jnp.exp(-bl)
    ko = cat(lambda i: rows(k, i) * jnp.exp(tot[i] - rows(bl, i)))

    def stack_keys(x):
        r = lax.broadcasted_iota(jnp.int32, (GLA_HEADS * c, w), 0) // c
        ln = lax.broadcasted_iota(jnp.int32, (GLA_HEADS * c, w), 1) // GLA_K_DIM
        return jnp.where(r == ln, jnp.concatenate([x] * GLA_HEADS, axis=0), 0.0).astype(BF16)

    qi = lax.broadcasted_iota(jnp.int32, (c, GLA_HEADS * c), 0)
    kj = lax.broadcasted_iota(jnp.int32, (c, GLA_HEADS * c), 1) % c
    same = (qi // GLA_SUB) == (kj // GLA_SUB)
    tri = (qi <= kj) if rev else (qi >= kj)
    a = jnp.where(same & tri, _dot_nt(qd.astype(BF16), stack_keys(kd)), 0.0)
    for j in order[:-1]:
        off = pre[j] + tot[j]
        qo = cat(lambda i: rows(q, i) * jnp.exp(rows(b_full, i) - off) if rank[i] > rank[j] else zeros)
        koj = cat(lambda i: rows(ko, i) if i == j else zeros)
        a = a + _dot_nt(qo.astype(BF16), stack_keys(koj))

    vr = lax.broadcasted_iota(jnp.int32, (GLA_HEADS * c, v.shape[1]), 0) // c
    vl = lax.broadcasted_iota(jnp.int32, (GLA_HEADS * c, v.shape[1]), 1) // GLA_V_DIM
    vst = jnp.where(vr == vl, jnp.concatenate([v] * GLA_HEADS, axis=0), 0.0).astype(BF16)
    o = _dot_nt(qe.astype(BF16), state.astype(BF16)) + jnp.dot(a.astype(BF16), vst, preferred_element_type=F32)

    sr = lax.broadcasted_iota(jnp.int32, state.shape, 0) // GLA_V_DIM
    sl = lax.broadcasted_iota(jnp.int32, state.shape, 1) // GLA_K_DIM
    upd = _dot_tn(v.astype(BF16), kdec.astype(BF16))
    new_state = state * jnp.exp(total) + jnp.where(sr == sl, upd, 0.0)
    return o, new_state


def _gla_kernel(q_ref, k_ref, v_ref, lr_ref, gw_ref, gb_ref, s0_ref, o_ref, sT_ref, s_scr, *, rev, nc):
    i = pl.program_id(1)

    @pl.when(i == 0)
    def _():
        s_scr[...] = s0_ref[...]

    logit = jnp.dot(lr_ref[...], gw_ref[...], preferred_element_type=F32) + gb_ref[...]
    g = (jnp.minimum(logit, 0.0) - jnp.log1p(jnp.exp(-jnp.abs(logit)))) * (1.0 / GLA_GATE_NORM)
    state = s_scr[...]
    for n in range(nc):
        c0 = ((nc - 1 - n) if rev else n) * GLA_CHUNK
        sl = slice(c0, c0 + GLA_CHUNK)
        o, state = _gla_chunk(q_ref[sl, :], k_ref[sl, :], v_ref[sl, :], g[sl], state, rev)
        o_ref[sl, :] = o
    s_scr[...] = state

    @pl.when(i == pl.num_programs(1) - 1)
    def _():
        sT_ref[...] = s_scr[...]


def _gla_call(p, gw, gb, s0, *, tb, rev, name):
    b, t, _ = p.shape
    nb = t // tb
    hk = GLA_HEADS * GLA_K_DIM
    hv = GLA_HEADS * GLA_V_DIM

    def tblk(i):
        return (nb - 1 - i) if rev else i

    return pl.pallas_call(
        functools.partial(_gla_kernel, rev=rev, nc=tb // GLA_CHUNK),
        grid=(b, nb),
        in_specs=[pl.BlockSpec((None, tb, hk), lambda bi, i: (bi, tblk(i), 0)),
                  pl.BlockSpec((None, tb, hk), lambda bi, i: (bi, tblk(i), 1)),
                  pl.BlockSpec((None, tb, hv), lambda bi, i: (bi, tblk(i), 1)),
                  pl.BlockSpec((None, tb, LANES), lambda bi, i: (bi, tblk(i), 2560 // LANES)),
                  pl.BlockSpec((LANES, hk), lambda bi, i: (0, 0)),
                  pl.BlockSpec((1, hk), lambda bi, i: (0, 0)),
                  pl.BlockSpec((None, hv, hk), lambda bi, i: (bi, 0, 0))],
        out_specs=[pl.BlockSpec((None, tb, hv), lambda bi, i: (bi, tblk(i), 0)),
                   pl.BlockSpec((None, hv, hk), lambda bi, i: (bi, 0, 0))],
        out_shape=[jax.ShapeDtypeStruct((b, t, hv), F32), jax.ShapeDtypeStruct((b, hv, hk), F32)],
        scratch_shapes=[pltpu.VMEM((hv, hk), F32)],
        compiler_params=_params("parallel", "arbitrary"),
        name=name,
    )(p, p, p, p, gw, gb, s0)


def _lru_kernel(zx_ref, zp_ref, zn_ref, cw_ref, cb_ref, wg_ref, bg_ref, lam_ref, h0_ref,
                h_ref, hT_ref, carry_scr, *, rev, nb):
    i = pl.program_id(1)
    blk = (nb - 1 - i) if rev else i

    @pl.when(i == 0)
    def _():
        carry_scr[...] = h0_ref[...]

    x = zx_ref[...]
    tb, w = x.shape
    row = lax.broadcasted_iota(jnp.int32, (tb, w), 0)
    pm = jnp.where(blk > 0, zp_ref[SUBLANES - 1:SUBLANES, :], 0.0)
    n0 = jnp.where(blk < nb - 1, zn_ref[0:1, :], 0.0)
    n1 = jnp.where(blk < nb - 1, zn_ref[1:2, :], 0.0)
    xm1 = jnp.where(row == 0, pm, pltpu.roll(x, 1, 0))
    xp1 = jnp.where(row == tb - 1, n0, pltpu.roll(x, tb - 1, 0))
    xp2 = jnp.where(row == tb - 1, n1, jnp.where(row == tb - 2, n0, pltpu.roll(x, tb - 2, 0)))
    cw = cw_ref[...]
    xr = xm1 * cw[0:1] + x * cw[1:2] + xp1 * cw[2:3] + xp2 * cw[3:4] + cb_ref[...]

    gm = jnp.dot(xr.astype(BF16), wg_ref[...], preferred_element_type=F32) + bg_ref[...]
    r = 0.5 * jnp.tanh(0.5 * gm[:, 0:w]) + 0.5
    ig = 0.5 * jnp.tanh(0.5 * gm[:, w:2 * w]) + 0.5
    log_a = (-LRU_C * r) * _softplus(-lam_ref[...])
    a = jnp.exp(log_a)
    u = jnp.sqrt(-jnp.tanh(log_a) * (a * a + 1.0)) * (ig * xr)

    sub = row % SUBLANES
    s = 1
    while s < SUBLANES:
        if rev:
            keep = sub < SUBLANES - s
            a_sh = jnp.where(keep, pltpu.roll(a, tb - s, 0), 1.0)
            u_sh = jnp.where(keep, pltpu.roll(u, tb - s, 0), 0.0)
        else:
            keep = sub >= s
            a_sh = jnp.where(keep, pltpu.roll(a, s, 0), 1.0)
            u_sh = jnp.where(keep, pltpu.roll(u, s, 0), 0.0)
        u = a * u_sh + u
        a = a * a_sh
        s *= 2
    h_prev = carry_scr[0:1, :]
    ng = tb // SUBLANES
    for gi in (range(ng - 1, -1, -1) if rev else range(ng)):
        rs = slice(gi * SUBLANES, (gi + 1) * SUBLANES)
        hg = u[rs] + a[rs] * h_prev
        h_ref[rs, :] = hg
        h_prev = hg[0:1] if rev else hg[SUBLANES - 1:SUBLANES]
    carry_scr[...] = jnp.broadcast_to(h_prev, carry_scr.shape)

    @pl.when(i == nb - 1)
    def _():
        hT_ref[...] = carry_scr[...]


def _lru_call(p, cw, cb, wg, bg, lam, h0, *, tb, rev, name):
    b, t, _ = p.shape
    nb = t // tb
    w = LRU_WIDTH
    zcol = 2048 // w
    per = tb // SUBLANES

    def tblk(i):
        return (nb - 1 - i) if rev else i

    return pl.pallas_call(
        functools.partial(_lru_kernel, rev=rev, nb=nb),
        grid=(b, nb),
        in_specs=[pl.BlockSpec((None, tb, w), lambda bi, i: (bi, tblk(i), zcol)),
                  pl.BlockSpec((None, SUBLANES, w), lambda bi, i: (bi, jnp.maximum(tblk(i) * per - 1, 0), zcol)),
                  pl.BlockSpec((None, SUBLANES, w),
                               lambda bi, i: (bi, jnp.minimum((tblk(i) + 1) * per, nb * per - 1), zcol)),
                  pl.BlockSpec((LRU_CONV, w), lambda bi, i: (0, 0)),
                  pl.BlockSpec((1, w), lambda bi, i: (0, 0)),
                  pl.BlockSpec((w, 2 * w), lambda bi, i: (0, 0)),
                  pl.BlockSpec((1, 2 * w), lambda bi, i: (0, 0)),
                  pl.BlockSpec((1, w), lambda bi, i: (0, 0)),
                  pl.BlockSpec((None, SUBLANES, w), lambda bi, i: (bi, 0, 0))],
        out_specs=[pl.BlockSpec((None, tb, w), lambda bi, i: (bi, tblk(i), 0)),
                   pl.BlockSpec((None, SUBLANES, w), lambda bi, i: (bi, 0, 0))],
        out_shape=[jax.ShapeDtypeStruct((b, t, w), F32), jax.ShapeDtypeStruct((b, SUBLANES, w), F32)],
        scratch_shapes=[pltpu.VMEM((SUBLANES, w), F32)],
        compiler_params=_params("parallel", "arbitrary"),
        name=name,
    )(p, p, p, cw, cb, wg, bg, lam, h0)


def _out_ffn_kernel(*refs, kind, final, f_chunk):
    if kind == 0:
        x_ref, ma_ref, mb_ref = refs[0:3]
        rest = refs[3:]
    else:
        x_ref, of_ref, ob_ref, hf_ref, hb_ref, gg_ref, zg_ref, glag_ref = refs[0:8]
        rest = refs[8:]
    if final:
        mod_ref, g2_ref, wo_ref, wi_ref, wf_ref, fg_ref, o_ref = rest
    else:
        mod_ref, g2_ref, wo_ref, wi_ref, wf_ref, o_ref = rest
    x = x_ref[...]
    d = x.shape[-1]
    if kind == 0:
        mix_a = ma_ref[...]
        mix_b = mb_ref[...]
    else:
        og = of_ref[...] + ob_ref[...]
        gg = gg_ref[...]
        parts = []
        for h in range(GLA_HEADS):
            hs = slice(h * GLA_V_DIM, (h + 1) * GLA_V_DIM)
            seg = og[:, hs]
            nrm = seg * lax.rsqrt(jnp.mean(seg * seg, axis=-1, keepdims=True) + NORM_EPS) * glag_ref[...]
            parts.append(nrm * _silu(gg[:, hs]))
        mix_a = jnp.concatenate(parts, axis=-1).astype(BF16)
        mix_b = ((hf_ref[...] + hb_ref[...]) * _gelu_tanh(zg_ref[...])).astype(BF16)
    wa = mix_a.shape[-1]
    y = (jnp.dot(mix_a, wo_ref[0:wa, :], preferred_element_type=F32)
         + jnp.dot(mix_b, wo_ref[wa:, :], preferred_element_type=F32))
    mod = mod_ref[...]
    x1 = x + mod[:, 2 * d:3 * d] * y
    h2 = _norm_mod(x1, g2_ref[...], mod[:, 3 * d:4 * d], mod[:, 4 * d:5 * d]).astype(BF16)
    f = wf_ref.shape[0]
    acc = None
    for f0 in range(0, f, f_chunk):
        gate = jnp.dot(h2, wi_ref[:, f0:f0 + f_chunk], preferred_element_type=F32)
        up = jnp.dot(h2, wi_ref[:, f + f0:f + f0 + f_chunk], preferred_element_type=F32)
        act = (_silu(gate) * up).astype(BF16)
        part = jnp.dot(act, wf_ref[f0:f0 + f_chunk, :], preferred_element_type=F32)
        acc = part if acc is None else acc + part
    x2 = x1 + mod[:, 5 * d:6 * d] * acc
    if final:
        ms = jnp.mean(x2 * x2, axis=-1, keepdims=True)
        x2 = x2 * lax.rsqrt(ms + NORM_EPS) * fg_ref[...]
    o_ref[...] = x2


def _out_ffn_call(x, mix, mod3, mod_row, g2, wo, wi, wf, final_g, *, kind, tm, f_chunk, name):
    b, t, d = x.shape
    f = wf.shape[0]

    def tile(width, col=0):
        return pl.BlockSpec((None, tm, width), lambda bi, i: (bi, i, col))

    in_specs = [tile(d)]
    args = [x]
    if kind == 0:
        ma, mb = mix
        in_specs += [tile(ma.shape[-1]), tile(mb.shape[-1])]
        args += [ma, mb]
    else:
        o_f, o_b, h_f, h_b, p, gla_g = mix
        hv = GLA_HEADS * GLA_V_DIM
        in_specs += [tile(hv), tile(hv), tile(LRU_WIDTH), tile(LRU_WIDTH),
                     tile(hv, 1024 // hv), tile(LRU_WIDTH, 1536 // LRU_WIDTH),
                     pl.BlockSpec((1, GLA_V_DIM), lambda bi, i: (0, 0))]
        args += [o_f, o_b, h_f, h_b, p, p, gla_g.reshape(1, GLA_V_DIM)]
    in_specs += [pl.BlockSpec((None, 1, mod3.shape[-1]), lambda bi, i: (mod_row(bi), 0, 0)),
                 pl.BlockSpec((1, d), lambda bi, i: (0, 0)),
                 _resident(wo.shape, lambda bi, i: (0, 0)),
                 _resident(wi.shape, lambda bi, i: (0, 0)),
                 _resident(wf.shape, lambda bi, i: (0, 0))]
    args += [mod3, g2.reshape(1, d), wo, wi, wf]
    final = final_g is not None
    if final:
        in_specs.append(pl.BlockSpec((1, d), lambda bi, i: (0, 0)))
        args.append(final_g.reshape(1, d))
    return pl.pallas_call(
        functools.partial(_out_ffn_kernel, kind=kind, final=final, f_chunk=f_chunk),
        grid=(b, t // tm),
        in_specs=in_specs,
        out_specs=pl.BlockSpec((None, tm, d), lambda bi, i: (bi, i, 0)),
        out_shape=jax.ShapeDtypeStruct((b, t, d), F32),
        compiler_params=_params("parallel", "parallel"),
        name=name,
    )(*args)


def _rope_table(n_tokens):
    n_rows = n_tokens // GRID_W
    row = jnp.broadcast_to(jnp.arange(n_rows)[:, None], (n_rows, GRID_W)).reshape(-1)
    col = jnp.broadcast_to(jnp.arange(GRID_W)[None, :], (n_rows, GRID_W)).reshape(-1)
    axis_dim = ROPE_DIM // 2
    inv_freq = ROPE_THETA ** (-jnp.arange(0, axis_dim, 2, dtype=F32) / axis_dim)
    ang_r = row.astype(F32)[:, None] * inv_freq
    ang_c = col.astype(F32)[:, None] * inv_freq
    cr, sr, cc, sc = jnp.cos(ang_r), jnp.sin(ang_r), jnp.cos(ang_c), jnp.sin(ang_c)
    cos = jnp.concatenate([cr, cr, cc, cc], axis=-1)
    sin = jnp.concatenate([-sr, sr, -sc, sc], axis=-1)
    reps = LANES // ROPE_DIM
    return jnp.concatenate([jnp.tile(cos, (1, reps)), jnp.tile(sin, (1, reps))], axis=-1)


def _even_w_in(w):
    qa, ka, va, qb, kb, vb = jnp.split(w, (512, 1024, 1536, 2048, 2176), axis=-1)
    qa_scale = DA_QK_DIM ** -0.5 * math.log2(math.e)
    qb_scale = WA_HEAD_DIM ** -0.5 * math.log2(math.e)
    return jnp.concatenate([qa * qa_scale, ka, va, qb * qb_scale, kb, vb], axis=-1).astype(BF16)


def _odd_w_in(w):
    q, k, v, g, lr, zg, zx = jnp.split(w, (256, 512, 1024, 1536, 1568, 2080), axis=-1)
    pad = jnp.zeros((w.shape[0], LANES - lr.shape[1]), w.dtype)
    return jnp.concatenate([q * GLA_K_DIM ** -0.5, k, v, g, zg, zx, lr, pad], axis=-1).astype(BF16)


def _block_diag(w):
    n, c, d = w.shape
    eye = jnp.eye(n, dtype=w.dtype)
    return (eye[:, None, :, None] * w[:, :, None, :]).reshape(n * c, n * d)


def _tile_rows(n, cap):
    t = min(n, cap)
    while n % t:
        t //= 2
    return t


def kernel(x, c, ctx, c_ctx, ada_w, ada_b, norm_g, even_w_in, even_w_out, diff_lam, win_sink, odd_w_in, odd_w_out, gla_gate_w, gla_gate_b, gla_norm_g, lru_conv_w, lru_conv_b, lru_wa, lru_ba, lru_wx, lru_bx, lru_lam, ffn_w_in, ffn_w_out, final_g):
    b, l, d = x.shape
    lc = ctx.shape[1]
    depth = ada_w.shape[0]
    assert b < SUBLANES and d == 1024

    cond = jnp.zeros((SUBLANES, d), F32).at[0:b].set(c).at[b].set(c_ctx)
    mod = _ada_call(cond, ada_w, ada_b)
    cs = _rope_table(l)

    xl = x
    xc = ctx.reshape(1, b * lc, d)
    tm_l = _tile_rows(l, 512)
    tm_c = _tile_rows(b * lc, 512)
    lat_row = lambda bi: bi
    ctx_row = lambda bi: b

    for li in range(depth):
        last = li == depth - 1
        mod3 = mod[li].reshape(SUBLANES, 1, 6 * d)
        wf_in = ffn_w_in[li].astype(BF16)
        wf_out = ffn_w_out[li].astype(BF16)
        fg = final_g if last else None
        f_chunk = wf_out.shape[0] // 2
        if li % 2 == 0:
            e = li // 2
            lam_init = 0.8 - 0.6 * math.exp(-0.3 * li)
            w_in = _even_w_in(even_w_in[e])
            w_out = even_w_out[e].astype(BF16)
            rope_groups = frozenset(range(0, 8)) | frozenset(range(12, 17))
            p_l = _in_proj_call(xl, mod3, lat_row, norm_g[li, 0], w_in, cs, tm=tm_l, n_chunk=768,
                                rope_groups=rope_groups, out_dtype=BF16, name=f"in_proj_l{li}")
            p_c = _in_proj_call(xc, mod3, ctx_row, norm_g[li, 0], w_in, None, tm=tm_c, n_chunk=768,
                                rope_groups=None, out_dtype=BF16, name=f"in_proj_c{li}")
            p_c = p_c.reshape(b, lc, EVEN_IN)
            tq = _tile_rows(l, 256)
            a_l = _diff_attn_call(diff_lam[e], p_l, p_c, p_l, tq=_tile_rows(l, 512), n_row_blocks=2, tk_cap=768,
                                  lam_init=lam_init, name=f"diff_attn_l{li}")
            b_l = _win_attn_call(win_sink[e], p_l, p_c, p_l, tq=tq, name=f"win_attn_l{li}")
            xl = _out_ffn_call(xl, (a_l, b_l), mod3, lat_row, norm_g[li, 1], w_out, wf_in, wf_out, fg,
                               kind=0, tm=tm_l, f_chunk=f_chunk, name=f"out_ffn_l{li}")
            if not last:
                a_c = _diff_attn_call(diff_lam[e], p_c, p_c, None, tq=lc, n_row_blocks=2, tk_cap=768,
                                      lam_init=lam_init, name=f"diff_attn_c{li}")
                b_c = _win_attn_call(win_sink[e], p_c, p_c, None, tq=lc, name=f"win_attn_c{li}")
                mix_c = (a_c.reshape(1, b * lc, -1), b_c.reshape(1, b * lc, -1))
                xc = _out_ffn_call(xc, mix_c, mod3, ctx_row, norm_g[li, 1], w_out, wf_in, wf_out, None,
                                   kind=0, tm=tm_c, f_chunk=f_chunk, name=f"out_ffn_c{li}")
        else:
            o = li // 2
            w_in = _odd_w_in(odd_w_in[o])
            w_out = odd_w_out[o].astype(BF16)
            p_l = _in_proj_call(xl, mod3, lat_row, norm_g[li, 0], w_in, None, tm=tm_l, n_chunk=896,
                                rope_groups=None, out_dtype=F32, name=f"in_proj_l{li}")
            p_c = _in_proj_call(xc, mod3, ctx_row, norm_g[li, 0], w_in, None, tm=tm_c, n_chunk=896,
                                rope_groups=None, out_dtype=F32, name=f"in_proj_c{li}")
            p_c = p_c.reshape(b, lc, ODD_IN_PAD)
            hk = GLA_HEADS * GLA_K_DIM
            hv = GLA_HEADS * GLA_V_DIM
            s0 = jnp.zeros((b, hv, hk), F32)
            h0 = jnp.zeros((b, SUBLANES, LRU_WIDTH), F32)
            cw = lru_conv_w[o]
            cb = lru_conv_b[o].reshape(1, LRU_WIDTH)
            outs = []
            for dr in range(2):
                rev = dr == 1
                gw = jnp.zeros((LANES, hk), F32).at[dr * GLA_GATE_RANK:(dr + 1) * GLA_GATE_RANK].set(gla_gate_w[o, dr])
                gb = gla_gate_b[o, dr].reshape(1, hk)
                o_c, s_c = _gla_call(p_c, gw, gb, s0, tb=_tile_rows(lc, 256), rev=rev, name=f"gla_c{li}d{dr}")
                o_l, _ = _gla_call(p_l, gw, gb, s_c, tb=_tile_rows(l, 512), rev=rev, name=f"gla_l{li}d{dr}")
                wg = jnp.concatenate([_block_diag(lru_wa[o, dr]), _block_diag(lru_wx[o, dr])], axis=-1).astype(BF16)
                bg = jnp.concatenate([lru_ba[o, dr], lru_bx[o, dr]]).reshape(1, 2 * LRU_WIDTH)
                lam = lru_lam[o, dr].reshape(1, LRU_WIDTH)
                h_c, hT = _lru_call(p_c, cw, cb, wg, bg, lam, h0, tb=_tile_rows(lc, 256), rev=rev,
                                    name=f"lru_c{li}d{dr}")
                h_l, _ = _lru_call(p_l, cw, cb, wg, bg, lam, hT, tb=_tile_rows(l, 256), rev=rev,
                                   name=f"lru_l{li}d{dr}")
                outs.append((o_c, o_l, h_c, h_l))
            mix_l = (outs[0][1], outs[1][1], outs[0][3], outs[1][3], p_l, gla_norm_g[o])
            xl = _out_ffn_call(xl, mix_l, mod3, lat_row, norm_g[li, 1], w_out, wf_in, wf_out, fg,
                               kind=1, tm=tm_l, f_chunk=f_chunk, name=f"out_ffn_l{li}")
            if not last:
                flat = lambda a: a.reshape(1, b * lc, a.shape[-1])
                mix_c = (flat(outs[0][0]), flat(outs[1][0]), flat(outs[0][2]), flat(outs[1][2]),
                         flat(p_c), gla_norm_g[o])
                xc = _out_ffn_call(xc, mix_c, mod3, ctx_row, norm_g[li, 1], w_out, wf_in, wf_out, None,
                                   kind=1, tm=tm_c, f_chunk=f_chunk, name=f"out_ffn_c{li}")
    return xl
```

```python
import functools
import math

import jax
import jax.numpy as jnp
from jax import lax
from jax.experimental import pallas as pl
from jax.experimental.pallas import tpu as pltpu

F32 = jnp.float32
BF16 = jnp.bfloat16

NORM_EPS = 1e-6
ROPE_THETA = 10000.0
ROPE_DIM = 64
GRID_W = 64
MASK_VALUE = -1e30

DA_HEADS = 4
DA_QK_DIM = 64
DA_V_DIM = 128
WA_Q_HEADS = 8
WA_KV_HEADS = 2
WA_GROUP = WA_Q_HEADS // WA_KV_HEADS
WA_HEAD_DIM = 64
WINDOW = 128

GLA_HEADS = 4
GLA_K_DIM = 64
GLA_V_DIM = 128
GLA_GATE_RANK = 16
GLA_GATE_NORM = 16.0
LRU_WIDTH = 512
LRU_BLOCKS = 8
LRU_CONV = 4
LRU_C = 8.0

LANES = 128
SUBLANES = 8
VMEM_LIMIT_BYTES = 56 * 1024 * 1024

GLA_CHUNK = 64
GLA_SUB = 16
EVEN_IN = 2304
ODD_IN_PAD = 2688


def _params(*sem):
    return pltpu.CompilerParams(dimension_semantics=sem, vmem_limit_bytes=VMEM_LIMIT_BYTES)


def _resident(block_shape, index_map):
    return pl.BlockSpec(block_shape, index_map, pipeline_mode=pl.Buffered(1))


def _softplus(x):
    return jnp.maximum(x, 0.0) + jnp.log1p(jnp.exp(-jnp.abs(x)))


def _silu(x):
    return x * jax.nn.sigmoid(x)


def _gelu_tanh(x):
    return 0.5 * x * (1.0 + jnp.tanh(math.sqrt(2.0 / math.pi) * (x + 0.044715 * (x * x * x))))


def _dot_nt(a, b):
    return lax.dot_general(a, b, (((1,), (1,)), ((), ())), preferred_element_type=F32)


def _dot_tn(a, b):
    return lax.dot_general(a, b, (((0,), (0,)), ((), ())), preferred_element_type=F32)


def _ada_kernel(cond_ref, w_ref, b_ref, o_ref):
    s = _silu(cond_ref[...])
    o_ref[...] = jnp.dot(s, w_ref[...], preferred_element_type=F32,
                         precision=lax.Precision.HIGHEST) + b_ref[...]


def _ada_call(cond, ada_w, ada_b):
    depth, d, n = ada_w.shape
    tn = 1536
    return pl.pallas_call(
        _ada_kernel,
        grid=(depth, n // tn),
        in_specs=[pl.BlockSpec((SUBLANES, d), lambda l, j: (0, 0)),
                  pl.BlockSpec((None, d, tn), lambda l, j: (l, 0, j)),
                  pl.BlockSpec((None, 1, tn), lambda l, j: (l, 0, j))],
        out_specs=pl.BlockSpec((None, SUBLANES, tn), lambda l, j: (l, 0, j)),
        out_shape=jax.ShapeDtypeStruct((depth, SUBLANES, n), F32),
        compiler_params=_params("arbitrary", "arbitrary"),
        name="ada_mod",
    )(cond, ada_w, ada_b.reshape(depth, 1, n))


def _norm_mod(x, g, shift, scale):
    ms = jnp.mean(x * x, axis=-1, keepdims=True)
    return (x * lax.rsqrt(ms + NORM_EPS) * g) * (1.0 + scale) + shift


def _in_proj_kernel(*refs, rope_groups, n_chunk):
    if rope_groups:
        x_ref, mod_ref, g_ref, w_ref, cs_ref, o_ref = refs
    else:
        x_ref, mod_ref, g_ref, w_ref, o_ref = refs
    x = x_ref[...]
    d = x.shape[-1]
    mod = mod_ref[...]
    hb = _norm_mod(x, g_ref[...], mod[:, 0:d], mod[:, d:2 * d]).astype(BF16)
    n = w_ref.shape[1]
    if rope_groups:
        cos = cs_ref[:, 0:LANES]
        sin = cs_ref[:, LANES:2 * LANES]
        lane = lax.broadcasted_iota(jnp.int32, (x.shape[0], LANES), 1)
        first_half = (lane % 32) < 16
    for n0 in range(0, n, n_chunk):
        nw = min(n_chunk, n - n0)
        acc = jnp.dot(hb, w_ref[:, n0:n0 + nw], preferred_element_type=F32)
        if not rope_groups:
            o_ref[:, n0:n0 + nw] = acc.astype(o_ref.dtype)
            continue
        for j in range(nw // LANES):
            grp = acc[:, j * LANES:(j + 1) * LANES]
            if (n0 // LANES + j) in rope_groups:
                partner = jnp.where(first_half, pltpu.roll(grp, LANES - 16, 1), pltpu.roll(grp, 16, 1))
                grp = grp * cos + partner * sin
            c0 = n0 + j * LANES
            o_ref[:, c0:c0 + LANES] = grp.astype(o_ref.dtype)


def _in_proj_call(x, mod3, mod_row, g, w, cs, *, tm, n_chunk, rope_groups, out_dtype, name):
    b, t, d = x.shape
    n = w.shape[1]
    in_specs = [pl.BlockSpec((None, tm, d), lambda bi, i: (bi, i, 0)),
                pl.BlockSpec((None, 1, mod3.shape[-1]), lambda bi, i: (mod_row(bi), 0, 0)),
                pl.BlockSpec((1, d), lambda bi, i: (0, 0)),
                _resident((d, n), lambda bi, i: (0, 0))]
    args = [x, mod3, g.reshape(1, d), w]
    if rope_groups:
        in_specs.append(pl.BlockSpec((tm, 2 * LANES), lambda bi, i: (i, 0)))
        args.append(cs)
    return pl.pallas_call(
        functools.partial(_in_proj_kernel, rope_groups=rope_groups, n_chunk=n_chunk),
        grid=(b, t // tm),
        in_specs=in_specs,
        out_specs=pl.BlockSpec((None, tm, n), lambda bi, i: (bi, i, 0)),
        out_shape=jax.ShapeDtypeStruct((b, t, n), out_dtype),
        compiler_params=_params("parallel", "parallel"),
        name=name,
    )(*args)


def _diff_attn_kernel(*refs, tk, has_lat, lam_init, n_row_blocks):
    if has_lat:
        lam_ref, q_ref, kc_ref, vc_ref, kl_ref, vl_ref, o_ref, kx, vx, s_a, s_b = refs
    else:
        lam_ref, q_ref, kc_ref, vc_ref, o_ref, kx, vx, s_a, s_b = refs
    lc = kc_ref.shape[0]
    t = kx.shape[0]

    @pl.when(pl.program_id(2) == 0)
    def _():
        kx[0:lc, :] = kc_ref[...]
        vx[0:lc, 0:DA_V_DIM] = vc_ref[...]
        if has_lat:
            kx[lc:t, :] = kl_ref[...]
            vx[lc:t, 0:DA_V_DIM] = vl_ref[...]
        lane_t = lax.broadcasted_iota(jnp.int32, (t, LANES), 1)
        vx[:, DA_V_DIM:DA_V_DIM + LANES] = jnp.where(lane_t == 0, 1.0, 0.0).astype(BF16)

    q = q_ref[...]
    tq = q.shape[0]
    lane = lax.broadcasted_iota(jnp.int32, q.shape, 1)
    zero = jnp.zeros_like(q)
    qmaps = (jnp.where(lane < DA_QK_DIM, q, zero), jnp.where(lane >= DA_QK_DIM, q, zero))
    rows = tq // n_row_blocks
    qm = [qmap[r * rows:(r + 1) * rows] for r in range(n_row_blocks) for qmap in qmaps]

    def chunk(j):
        return pl.ds(j * tk if isinstance(j, int) else pl.multiple_of(j * tk, tk), tk)

    def scores_into(buf, j):
        k = kx[chunk(j), :]
        for ci in range(len(qm)):
            buf[ci] = _dot_nt(qm[ci], k)

    def consume(buf, j, carry):
        v = vx[chunk(j), :]
        out = []
        for ci in range(len(qm)):
            m, acc = carry[ci]
            s = buf[ci]
            m_new = jnp.maximum(m, jnp.max(s, axis=-1, keepdims=True))
            p = jnp.exp2(s - m_new)
            acc = jnp.exp2(m - m_new) * acc + jnp.dot(p.astype(BF16), v, preferred_element_type=F32)
            out.append((m_new, acc))
        return tuple(out)

    n = t // tk
    n_pairs = (n - 1) // 2
    buf_a, buf_b = s_a, s_b
    scores_into(buf_a, 0)

    def body(i, carry):
        scores_into(buf_b, 2 * i + 1)
        carry = consume(buf_a, 2 * i, carry)
        scores_into(buf_a, 2 * i + 2)
        return consume(buf_b, 2 * i + 1, carry)

    carry = tuple((jnp.full((rows, 1), MASK_VALUE, F32), jnp.zeros((rows, DA_V_DIM + LANES), F32)) for _ in qm)
    for i in range(n_pairs):
        carry = body(i, carry)
    if n % 2 == 0:
        scores_into(buf_b, n - 1)
        carry = consume(buf_a, n - 2, carry)
        carry = consume(buf_b, n - 1, carry)
    else:
        carry = consume(buf_a, n - 1, carry)
    o = [acc[:, 0:DA_V_DIM] / acc[:, DA_V_DIM:DA_V_DIM + 1] for _, acc in carry]
    lv = lam_ref[...]
    lam = (jnp.exp(jnp.sum(lv[0:1] * lv[1:2], axis=-1, keepdims=True))
           - jnp.exp(jnp.sum(lv[2:3] * lv[3:4], axis=-1, keepdims=True)) + lam_init)
    for r in range(n_row_blocks):
        w = o[2 * r] - lam * o[2 * r + 1]
        y = w * lax.rsqrt(jnp.mean(w * w, axis=-1, keepdims=True) + NORM_EPS) * (1.0 - lam_init)
        o_ref[r * rows:(r + 1) * rows, :] = y.astype(o_ref.dtype)


def _diff_attn_call(lam_vec, pq, pc, pl_kv, *, tq, n_row_blocks, tk_cap, lam_init, name):
    b, t, _ = pq.shape
    lc = pc.shape[1]
    kcol, vcol = DA_HEADS, 2 * DA_HEADS
    in_specs = [pl.BlockSpec((4, DA_QK_DIM), lambda bi, h, i: (0, 0)),
                pl.BlockSpec((None, tq, LANES), lambda bi, h, i: (bi, i, h)),
                pl.BlockSpec((None, lc, LANES), lambda bi, h, i: (bi, 0, kcol + h)),
                pl.BlockSpec((None, lc, LANES), lambda bi, h, i: (bi, 0, vcol + h))]
    args = [lam_vec, pq, pc, pc]
    n_keys = lc
    if pl_kv is not None:
        l = pl_kv.shape[1]
        n_keys += l
        in_specs += [pl.BlockSpec((None, l, LANES), lambda bi, h, i: (bi, 0, kcol + h)),
                     pl.BlockSpec((None, l, LANES), lambda bi, h, i: (bi, 0, vcol + h))]
        args += [pl_kv, pl_kv]
    tk = max(c for c in range(LANES, tk_cap + 1, LANES) if n_keys % c == 0)
    return pl.pallas_call(
        functools.partial(_diff_attn_kernel, tk=tk, has_lat=pl_kv is not None, lam_init=lam_init,
                          n_row_blocks=n_row_blocks),
        grid=(b, DA_HEADS, t // tq),
        in_specs=in_specs,
        out_specs=pl.BlockSpec((None, tq, LANES), lambda bi, h, i: (bi, i, h)),
        out_shape=jax.ShapeDtypeStruct((b, t, DA_HEADS * DA_V_DIM), BF16),
        scratch_shapes=[pltpu.VMEM((n_keys, LANES), BF16), pltpu.VMEM((n_keys, DA_V_DIM + LANES), BF16),
                        pltpu.VMEM((2 * n_row_blocks, tq // n_row_blocks, tk), F32),
                        pltpu.VMEM((2 * n_row_blocks, tq // n_row_blocks, tk), F32)],
        compiler_params=_params("arbitrary", "arbitrary", "arbitrary"),
        name=name,
    )(*args)


def _win_attn_kernel(*refs, tq, span, has_win):
    if has_win:
        sink_ref, q_ref, kc_ref, vc_ref, kl_ref, vl_ref, o_ref = refs
    else:
        sink_ref, q_ref, kc_ref, vc_ref, o_ref = refs
    q = q_ref[...]
    kc = kc_ref[...]
    vc = vc_ref[...]
    if has_win:
        l = kl_ref.shape[0]
        q0 = pl.program_id(1) * tq
        start = pl.multiple_of(jnp.clip(q0 - WINDOW, 0, l - span), WINDOW)
        kw = kl_ref[pl.ds(start, span), :]
        vw = vl_ref[pl.ds(start, span), :]
        qpos = q0 + lax.broadcasted_iota(jnp.int32, (tq, span), 0)
        kpos = start + lax.broadcasted_iota(jnp.int32, (tq, span), 1)
        valid = jnp.abs(kpos - qpos) <= WINDOW
    for hk in range(WA_KV_HEADS):
        ks = slice(hk * WA_HEAD_DIM, (hk + 1) * WA_HEAD_DIM)
        kc_h, vc_h = kc[:, ks], vc[:, ks]
        if has_win:
            kw_h, vw_h = kw[:, ks], vw[:, ks]
        for g in range(WA_GROUP):
            h = hk * WA_GROUP + g
            hs = slice(h * WA_HEAD_DIM, (h + 1) * WA_HEAD_DIM)
            q_h = q[:, hs]
            sink = sink_ref[h] * math.log2(math.e)
            s_c = _dot_nt(q_h, kc_h)
            m = jnp.maximum(jnp.max(s_c, axis=-1, keepdims=True), sink)
            if has_win:
                s_w = jnp.where(valid, _dot_nt(q_h, kw_h), MASK_VALUE)
                m = jnp.maximum(m, jnp.max(s_w, axis=-1, keepdims=True))
            p_c = jnp.exp2(s_c - m)
            den = jnp.sum(p_c, axis=-1, keepdims=True) + jnp.exp2(sink - m)
            num = jnp.dot(p_c.astype(BF16), vc_h, preferred_element_type=F32)
            if has_win:
                p_w = jnp.exp2(s_w - m)
                den = den + jnp.sum(p_w, axis=-1, keepdims=True)
                num = num + jnp.dot(p_w.astype(BF16), vw_h, preferred_element_type=F32)
            o_ref[:, hs] = (num / den).astype(o_ref.dtype)


def _win_attn_call(sink, pq, pc, pl_kv, *, tq, name):
    b, t, _ = pq.shape
    lc = pc.shape[1]
    qcol = 3
    kcol, vcol = 16, 17
    in_specs = [pl.BlockSpec(memory_space=pltpu.SMEM),
                pl.BlockSpec((None, tq, 4 * LANES), lambda bi, i: (bi, i, qcol)),
                pl.BlockSpec((None, lc, LANES), lambda bi, i: (bi, 0, kcol)),
                pl.BlockSpec((None, lc, LANES), lambda bi, i: (bi, 0, vcol))]
    args = [sink, pq, pc, pc]
    has_win = pl_kv is not None
    if has_win:
        l = pl_kv.shape[1]
        in_specs += [pl.BlockSpec((None, l, LANES), lambda bi, i: (bi, 0, kcol)),
                     pl.BlockSpec((None, l, LANES), lambda bi, i: (bi, 0, vcol))]
        args += [pl_kv, pl_kv]
    return pl.pallas_call(
        functools.partial(_win_attn_kernel, tq=tq, span=tq + 2 * WINDOW, has_win=has_win),
        grid=(b, t // tq),
        in_specs=in_specs,
        out_specs=pl.BlockSpec((None, tq, 4 * LANES), lambda bi, i: (bi, i, 0)),
        out_shape=jax.ShapeDtypeStruct((b, t, WA_Q_HEADS * WA_HEAD_DIM), BF16),
        compiler_params=_params("parallel", "arbitrary"),
        name=name,
    )(*args)


def _gla_chunk(q, k, v, g, state, rev):
    c, w = q.shape
    nsub = c // GLA_SUB
    row = lax.broadcasted_iota(jnp.int32, (c, w), 0)
    rb = row % GLA_SUB
    bl = g
    s = 1
    while s < GLA_SUB:
        if rev:
            bl = bl + jnp.where(rb < GLA_SUB - s, pltpu.roll(bl, c - s, 0), 0.0)
        else:
            bl = bl + jnp.where(rb >= s, pltpu.roll(bl, s, 0), 0.0)
        s *= 2
    order = list(range(nsub))[::-1] if rev else list(range(nsub))
    rank = {blk: n for n, blk in enumerate(order)}

    def rows(a, i):
        return a[i * GLA_SUB:(i + 1) * GLA_SUB]

    tot = {}
    for i in range(nsub):
        r = i * GLA_SUB if rev else i * GLA_SUB + GLA_SUB - 1
        tot[i] = bl[r:r + 1]
    pre = {}
    run = jnp.zeros((1, w), F32)
    for i in order:
        pre[i] = run
        run = run + tot[i]
    total = run

    def cat(fn):
        return jnp.concatenate([fn(i) for i in range(nsub)], axis=0)

    zeros = jnp.zeros((GLA_SUB, w), F32)
    b_full = cat(lambda i: rows(bl, i) + pre[i])
    qe = q * jnp.exp(b_full)
    kdec = k * jnp.exp(total - b_full)
    qd = q * jnp.exp(bl)
    kd = k * jnp.exp(-bl)
    ko = cat(lambda i: rows(k, i) * jnp.exp(tot[i] - rows(bl, i)))

    def stack_keys(x):
        r = lax.broadcasted_iota(jnp.int32, (GLA_HEADS * c, w), 0) // c
        ln = lax.broadcasted_iota(jnp.int32, (GLA_HEADS * c, w), 1) // GLA_K_DIM
        return jnp.where(r == ln, jnp.concatenate([x] * GLA_HEADS, axis=0), 0.0).astype(BF16)

    qi = lax.broadcasted_iota(jnp.int32, (c, GLA_HEADS * c), 0)
    kj = lax.broadcasted_iota(jnp.int32, (c, GLA_HEADS * c), 1) % c
    same = (qi // GLA_SUB) == (kj // GLA_SUB)
    tri = (qi <= kj) if rev else (qi >= kj)
    a = jnp.where(same & tri, _dot_nt(qd.astype(BF16), stack_keys(kd)), 0.0)
    for j in order[:-1]:
        off = pre[j] + tot[j]
        qo = cat(lambda i: rows(q, i) * jnp.exp(rows(b_full, i) - off) if rank[i] > rank[j] else zeros)
        koj = cat(lambda i: rows(ko, i) if i == j else zeros)
        a = a + _dot_nt(qo.astype(BF16), stack_keys(koj))

    vr = lax.broadcasted_iota(jnp.int32, (GLA_HEADS * c, v.shape[1]), 0) // c
    vl = lax.broadcasted_iota(jnp.int32, (GLA_HEADS * c, v.shape[1]), 1) // GLA_V_DIM
    vst = jnp.where(vr == vl, jnp.concatenate([v] * GLA_HEADS, axis=0), 0.0).astype(BF16)
    o = _dot_nt(qe.astype(BF16), state.astype(BF16)) + jnp.dot(a.astype(BF16), vst, preferred_element_type=F32)

    sr = lax.broadcasted_iota(jnp.int32, state.shape, 0) // GLA_V_DIM
    sl = lax.broadcasted_iota(jnp.int32, state.shape, 1) // GLA_K_DIM
    upd = _dot_tn(v.astype(BF16), kdec.astype(BF16))
    new_state = state * jnp.exp(total) + jnp.where(sr == sl, upd, 0.0)
    return o, new_state


def _gla_kernel(q_ref, k_ref, v_ref, lr_ref, gw_ref, gb_ref, s0_ref, o_ref, sT_ref, s_scr, *, rev, nc):
    i = pl.program_id(1)
    nbatch = q_ref.shape[0]

    @pl.when(i == 0)
    def _():
        s_scr[...] = s0_ref[...]

    gates, states = [], []
    for bb in range(nbatch):
        logit = jnp.dot(lr_ref[bb], gw_ref[...], preferred_element_type=F32) + gb_ref[...]
        gates.append((jnp.minimum(logit, 0.0) - jnp.log1p(jnp.exp(-jnp.abs(logit)))) * (1.0 / GLA_GATE_NORM))
        states.append(s_scr[bb])
    for n in range(nc):
        c0 = ((nc - 1 - n) if rev else n) * GLA_CHUNK
        sl = slice(c0, c0 + GLA_CHUNK)
        for bb in range(nbatch):
            o, states[bb] = _gla_chunk(q_ref[bb, sl, :], k_ref[bb, sl, :], v_ref[bb, sl, :], gates[bb][sl],
                                       states[bb], rev)
            o_ref[bb, sl, :] = o
    for bb in range(nbatch):
        s_scr[bb] = states[bb]

    @pl.when(i == pl.num_programs(1) - 1)
    def _():
        sT_ref[...] = s_scr[...]


def _gla_call(p, gw, gb, s0, *, tb, rev, name):
    b, t, _ = p.shape
    nb = t // tb
    hk = GLA_HEADS * GLA_K_DIM
    hv = GLA_HEADS * GLA_V_DIM
    bb = 2 if b % 2 == 0 else 1

    def tblk(i):
        return (nb - 1 - i) if rev else i

    return pl.pallas_call(
        functools.partial(_gla_kernel, rev=rev, nc=tb // GLA_CHUNK),
        grid=(b // bb, nb),
        in_specs=[pl.BlockSpec((bb, tb, hk), lambda bi, i: (bi, tblk(i), 0)),
                  pl.BlockSpec((bb, tb, hk), lambda bi, i: (bi, tblk(i), 1)),
                  pl.BlockSpec((bb, tb, hv), lambda bi, i: (bi, tblk(i), 1)),
                  pl.BlockSpec((bb, tb, LANES), lambda bi, i: (bi, tblk(i), 2560 // LANES)),
                  pl.BlockSpec((LANES, hk), lambda bi, i: (0, 0)),
                  pl.BlockSpec((1, hk), lambda bi, i: (0, 0)),
                  pl.BlockSpec((bb, hv, hk), lambda bi, i: (bi, 0, 0))],
        out_specs=[pl.BlockSpec((bb, tb, hv), lambda bi, i: (bi, tblk(i), 0)),
                   pl.BlockSpec((bb, hv, hk), lambda bi, i: (bi, 0, 0))],
        out_shape=[jax.ShapeDtypeStruct((b, t, hv), F32), jax.ShapeDtypeStruct((b, hv, hk), F32)],
        scratch_shapes=[pltpu.VMEM((bb, hv, hk), F32)],
        compiler_params=_params("parallel", "arbitrary"),
        name=name,
    )(p, p, p, p, gw, gb, s0)


def _lru_kernel(zx_ref, zp_ref, zn_ref, cw_ref, cb_ref, wg_ref, bg_ref, lam_ref, h0_ref,
                h_ref, hT_ref, carry_scr, *, rev, nb):
    i = pl.program_id(1)
    blk = (nb - 1 - i) if rev else i

    @pl.when(i == 0)
    def _():
        carry_scr[...] = h0_ref[...]

    x = zx_ref[...]
    tb, w = x.shape
    row = lax.broadcasted_iota(jnp.int32, (tb, w), 0)
    pm = jnp.where(blk > 0, zp_ref[SUBLANES - 1:SUBLANES, :], 0.0)
    n0 = jnp.where(blk < nb - 1, zn_ref[0:1, :], 0.0)
    n1 = jnp.where(blk < nb - 1, zn_ref[1:2, :], 0.0)
    xm1 = jnp.where(row == 0, pm, pltpu.roll(x, 1, 0))
    xp1 = jnp.where(row == tb - 1, n0, pltpu.roll(x, tb - 1, 0))
    xp2 = jnp.where(row == tb - 1, n1, jnp.where(row == tb - 2, n0, pltpu.roll(x, tb - 2, 0)))
    cw = cw_ref[...]
    xr = xm1 * cw[0:1] + x * cw[1:2] + xp1 * cw[2:3] + xp2 * cw[3:4] + cb_ref[...]

    gm = jnp.dot(xr.astype(BF16), wg_ref[...], preferred_element_type=F32) + bg_ref[...]
    r = 0.5 * jnp.tanh(0.5 * gm[:, 0:w]) + 0.5
    ig = 0.5 * jnp.tanh(0.5 * gm[:, w:2 * w]) + 0.5
    log_a = (-LRU_C * r) * _softplus(-lam_ref[...])
    a = jnp.exp(log_a)
    u = jnp.sqrt(-jnp.tanh(log_a) * (a * a + 1.0)) * (ig * xr)

    sub = lax.broadcasted_iota(jnp.int32, (SUBLANES, w), 0)
    h_prev = carry_scr[0:1, :]
    ng = tb // SUBLANES
    for gi in (range(ng - 1, -1, -1) if rev else range(ng)):
        rs = slice(gi * SUBLANES, (gi + 1) * SUBLANES)
        ag, ug = a[rs], u[rs]
        s = 1
        while s < SUBLANES:
            if rev:
                keep = sub < SUBLANES - s
                a_sh = jnp.where(keep, pltpu.roll(ag, SUBLANES - s, 0), 1.0)
                u_sh = jnp.where(keep, pltpu.roll(ug, SUBLANES - s, 0), 0.0)
            else:
                keep = sub >= s
                a_sh = jnp.where(keep, pltpu.roll(ag, s, 0), 1.0)
                u_sh = jnp.where(keep, pltpu.roll(ug, s, 0), 0.0)
            ug = ag * u_sh + ug
            ag = ag * a_sh
            s *= 2
        hg = ug + ag * h_prev
        h_ref[rs, :] = hg
        h_prev = hg[0:1] if rev else hg[SUBLANES - 1:SUBLANES]
    carry_scr[...] = jnp.broadcast_to(h_prev, carry_scr.shape)

    @pl.when(i == nb - 1)
    def _():
        hT_ref[...] = carry_scr[...]


def _lru_call(p, cw, cb, wg, bg, lam, h0, *, tb, rev, name):
    b, t, _ = p.shape
    nb = t // tb
    w = LRU_WIDTH
    zcol = 2048 // w
    per = tb // SUBLANES

    def tblk(i):
        return (nb - 1 - i) if rev else i

    return pl.pallas_call(
        functools.partial(_lru_kernel, rev=rev, nb=nb),
        grid=(b, nb),
        in_specs=[pl.BlockSpec((None, tb, w), lambda bi, i: (bi, tblk(i), zcol)),
                  pl.BlockSpec((None, SUBLANES, w), lambda bi, i: (bi, jnp.maximum(tblk(i) * per - 1, 0), zcol)),
                  pl.BlockSpec((None, SUBLANES, w),
                               lambda bi, i: (bi, jnp.minimum((tblk(i) + 1) * per, nb * per - 1), zcol)),
                  pl.BlockSpec((LRU_CONV, w), lambda bi, i: (0, 0)),
                  pl.BlockSpec((1, w), lambda bi, i: (0, 0)),
                  pl.BlockSpec((w, 2 * w), lambda bi, i: (0, 0)),
                  pl.BlockSpec((1, 2 * w), lambda bi, i: (0, 0)),
                  pl.BlockSpec((1, w), lambda bi, i: (0, 0)),
                  pl.BlockSpec((None, SUBLANES, w), lambda bi, i: (bi, 0, 0))],
        out_specs=[pl.BlockSpec((None, tb, w), lambda bi, i: (bi, tblk(i), 0)),
                   pl.BlockSpec((None, SUBLANES, w), lambda bi, i: (bi, 0, 0))],
        out_shape=[jax.ShapeDtypeStruct((b, t, w), F32), jax.ShapeDtypeStruct((b, SUBLANES, w), F32)],
        scratch_shapes=[pltpu.VMEM((SUBLANES, w), F32)],
        compiler_params=_params("parallel", "arbitrary"),
        name=name,
    )(p, p, p, cw, cb, wg, bg, lam, h0)


def _out_ffn_kernel(*refs, kind, final, f_chunk):
    if kind == 0:
        x_ref, ma_ref, mb_ref = refs[0:3]
        rest = refs[3:]
    else:
        x_ref, of_ref, ob_ref, hf_ref, hb_ref, gg_ref, zg_ref, glag_ref = refs[0:8]
        rest = refs[8:]
    if final:
        mod_ref, g2_ref, wo_ref, wi_ref, wf_ref, fg_ref, o_ref = rest
    else:
        mod_ref, g2_ref, wo_ref, wi_ref, wf_ref, o_ref = rest
    x = x_ref[...]
    d = x.shape[-1]
    if kind == 0:
        mix_a = ma_ref[...]
        mix_b = mb_ref[...]
    else:
        og = of_ref[...] + ob_ref[...]
        gg = gg_ref[...]
        parts = []
        for h in range(GLA_HEADS):
            hs = slice(h * GLA_V_DIM, (h + 1) * GLA_V_DIM)
            seg = og[:, hs]
            nrm = seg * lax.rsqrt(jnp.mean(seg * seg, axis=-1, keepdims=True) + NORM_EPS) * glag_ref[...]
            parts.append(nrm * _silu(gg[:, hs]))
        mix_a = jnp.concatenate(parts, axis=-1).astype(BF16)
        mix_b = ((hf_ref[...] + hb_ref[...]) * _gelu_tanh(zg_ref[...])).astype(BF16)
    wa = mix_a.shape[-1]
    y = (jnp.dot(mix_a, wo_ref[0:wa, :], preferred_element_type=F32)
         + jnp.dot(mix_b, wo_ref[wa:, :], preferred_element_type=F32))
    mod = mod_ref[...]
    x1 = x + mod[:, 2 * d:3 * d] * y
    h2 = _norm_mod(x1, g2_ref[...], mod[:, 3 * d:4 * d], mod[:, 4 * d:5 * d]).astype(BF16)
    f = wf_ref.shape[0]
    acc = None
    for f0 in range(0, f, f_chunk):
        gate = jnp.dot(h2, wi_ref[:, f0:f0 + f_chunk], preferred_element_type=F32)
        up = jnp.dot(h2, wi_ref[:, f + f0:f + f0 + f_chunk], preferred_element_type=F32)
        act = (_silu(gate) * up).astype(BF16)
        part = jnp.dot(act, wf_ref[f0:f0 + f_chunk, :], preferred_element_type=F32)
        acc = part if acc is None else acc + part
    x2 = x1 + mod[:, 5 * d:6 * d] * acc
    if final:
        ms = jnp.mean(x2 * x2, axis=-1, keepdims=True)
        x2 = x2 * lax.rsqrt(ms + NORM_EPS) * fg_ref[...]
    o_ref[...] = x2


def _out_ffn_call(x, mix, mod3, mod_row, g2, wo, wi, wf, final_g, *, kind, tm, f_chunk, name):
    b, t, d = x.shape
    f = wf.shape[0]

    def tile(width, col=0):
        return pl.BlockSpec((None, tm, width), lambda bi, i: (bi, i, col))

    in_specs = [tile(d)]
    args = [x]
    if kind == 0:
        ma, mb = mix
        in_specs += [tile(ma.shape[-1]), tile(mb.shape[-1])]
        args += [ma, mb]
    else:
        o_f, o_b, h_f, h_b, p, gla_g = mix
        hv = GLA_HEADS * GLA_V_DIM
        in_specs += [tile(hv), tile(hv), tile(LRU_WIDTH), tile(LRU_WIDTH),
                     tile(hv, 1024 // hv), tile(LRU_WIDTH, 1536 // LRU_WIDTH),
                     pl.BlockSpec((1, GLA_V_DIM), lambda bi, i: (0, 0))]
        args += [o_f, o_b, h_f, h_b, p, p, gla_g.reshape(1, GLA_V_DIM)]
    in_specs += [pl.BlockSpec((None, 1, mod3.shape[-1]), lambda bi, i: (mod_row(bi), 0, 0)),
                 pl.BlockSpec((1, d), lambda bi, i: (0, 0)),
                 _resident(wo.shape, lambda bi, i: (0, 0)),
                 _resident(wi.shape, lambda bi, i: (0, 0)),
                 _resident(wf.shape, lambda bi, i: (0, 0))]
    args += [mod3, g2.reshape(1, d), wo, wi, wf]
    final = final_g is not None
    if final:
        in_specs.append(pl.BlockSpec((1, d), lambda bi, i: (0, 0)))
        args.append(final_g.reshape(1, d))
    return pl.pallas_call(
        functools.partial(_out_ffn_kernel, kind=kind, final=final, f_chunk=f_chunk),
        grid=(b, t // tm),
        in_specs=in_specs,
        out_specs=pl.BlockSpec((None, tm, d), lambda bi, i: (bi, i, 0)),
        out_shape=jax.ShapeDtypeStruct((b, t, d), F32),
        compiler_params=_params("parallel", "parallel"),
        name=name,
    )(*args)


def _rope_table(n_tokens):
    n_rows = n_tokens // GRID_W
    row = jnp.broadcast_to(jnp.arange(n_rows)[:, None], (n_rows, GRID_W)).reshape(-1)
    col = jnp.broadcast_to(jnp.arange(GRID_W)[None, :], (n_rows, GRID_W)).reshape(-1)
    axis_dim = ROPE_DIM // 2
    inv_freq = ROPE_THETA ** (-jnp.arange(0, axis_dim, 2, dtype=F32) / axis_dim)
    ang_r = row.astype(F32)[:, None] * inv_freq
    ang_c = col.astype(F32)[:, None] * inv_freq
    cr, sr, cc, sc = jnp.cos(ang_r), jnp.sin(ang_r), jnp.cos(ang_c), jnp.sin(ang_c)
    cos = jnp.concatenate([cr, cr, cc, cc], axis=-1)
    sin = jnp.concatenate([-sr, sr, -sc, sc], axis=-1)
    reps = LANES // ROPE_DIM
    return jnp.concatenate([jnp.tile(cos, (1, reps)), jnp.tile(sin, (1, reps))], axis=-1)


def _even_w_in(w):
    qa, ka, va, qb, kb, vb = jnp.split(w, (512, 1024, 1536, 2048, 2176), axis=-1)
    qa_scale = DA_QK_DIM ** -0.5 * math.log2(math.e)
    qb_scale = WA_HEAD_DIM ** -0.5 * math.log2(math.e)
    return jnp.concatenate([qa * qa_scale, ka, va, qb * qb_scale, kb, vb], axis=-1).astype(BF16)


def _odd_w_in(w):
    q, k, v, g, lr, zg, zx = jnp.split(w, (256, 512, 1024, 1536, 1568, 2080), axis=-1)
    pad = jnp.zeros((w.shape[0], LANES - lr.shape[1]), w.dtype)
    return jnp.concatenate([q * GLA_K_DIM ** -0.5, k, v, g, zg, zx, lr, pad], axis=-1).astype(BF16)


def _block_diag(w):
    n, c, d = w.shape
    eye = jnp.eye(n, dtype=w.dtype)
    return (eye[:, None, :, None] * w[:, :, None, :]).reshape(n * c, n * d)


def _tile_rows(n, cap):
    t = min(n, cap)
    while n % t:
        t //= 2
    return t


def kernel(x, c, ctx, c_ctx, ada_w, ada_b, norm_g, even_w_in, even_w_out, diff_lam, win_sink, odd_w_in, odd_w_out, gla_gate_w, gla_gate_b, gla_norm_g, lru_conv_w, lru_conv_b, lru_wa, lru_ba, lru_wx, lru_bx, lru_lam, ffn_w_in, ffn_w_out, final_g):
    b, l, d = x.shape
    lc = ctx.shape[1]
    depth = ada_w.shape[0]
    assert b < SUBLANES and d == 1024

    cond = jnp.zeros((SUBLANES, d), F32).at[0:b].set(c).at[b].set(c_ctx)
    mod = _ada_call(cond, ada_w, ada_b)
    cs = _rope_table(l)

    xl = x
    xc = ctx.reshape(1, b * lc, d)
    tm_l = _tile_rows(l, 512)
    tm_c = _tile_rows(b * lc, 512)
    lat_row = lambda bi: bi
    ctx_row = lambda bi: b

    for li in range(depth):
        last = li == depth - 1
        mod3 = mod[li].reshape(SUBLANES, 1, 6 * d)
        wf_in = ffn_w_in[li].astype(BF16)
        wf_out = ffn_w_out[li].astype(BF16)
        fg = final_g if last else None
        f_chunk = wf_out.shape[0] // 2
        if li % 2 == 0:
            e = li // 2
            lam_init = 0.8 - 0.6 * math.exp(-0.3 * li)
            w_in = _even_w_in(even_w_in[e])
            w_out = even_w_out[e].astype(BF16)
            rope_groups = frozenset(range(0, 8)) | frozenset(range(12, 17))
            p_l = _in_proj_call(xl, mod3, lat_row, norm_g[li, 0], w_in, cs, tm=tm_l, n_chunk=768,
                                rope_groups=rope_groups, out_dtype=BF16, name=f"in_proj_l{li}")
            p_c = _in_proj_call(xc, mod3, ctx_row, norm_g[li, 0], w_in, None, tm=tm_c, n_chunk=768,
                                rope_groups=None, out_dtype=BF16, name=f"in_proj_c{li}")
            p_c = p_c.reshape(b, lc, EVEN_IN)
            tq = _tile_rows(l, 256)
            a_l = _diff_attn_call(diff_lam[e], p_l, p_c, p_l, tq=_tile_rows(l, 1024), n_row_blocks=4, tk_cap=768,
                                  lam_init=lam_init, name=f"diff_attn_l{li}")
            b_l = _win_attn_call(win_sink[e], p_l, p_c, p_l, tq=tq, name=f"win_attn_l{li}")
            xl = _out_ffn_call(xl, (a_l, b_l), mod3, lat_row, norm_g[li, 1], w_out, wf_in, wf_out, fg,
                               kind=0, tm=tm_l, f_chunk=f_chunk, name=f"out_ffn_l{li}")
            if not last:
                a_c = _diff_attn_call(diff_lam[e], p_c, p_c, None, tq=lc, n_row_blocks=2, tk_cap=768,
                                      lam_init=lam_init, name=f"diff_attn_c{li}")
                b_c = _win_attn_call(win_sink[e], p_c, p_c, None, tq=lc, name=f"win_attn_c{li}")
                mix_c = (a_c.reshape(1, b * lc, -1), b_c.reshape(1, b * lc, -1))
                xc = _out_ffn_call(xc, mix_c, mod3, ctx_row, norm_g[li, 1], w_out, wf_in, wf_out, None,
                                   kind=0, tm=tm_c, f_chunk=f_chunk, name=f"out_ffn_c{li}")
        else:
            o = li // 2
            w_in = _odd_w_in(odd_w_in[o])
            w_out = odd_w_out[o].astype(BF16)
            p_l = _in_proj_call(xl, mod3, lat_row, norm_g[li, 0], w_in, None, tm=tm_l, n_chunk=768,
                                rope_groups=None, out_dtype=F32, name=f"in_proj_l{li}")
            p_c = _in_proj_call(xc, mod3, ctx_row, norm_g[li, 0], w_in, None, tm=tm_c, n_chunk=768,
                                rope_groups=None, out_dtype=F32, name=f"in_proj_c{li}")
            p_c = p_c.reshape(b, lc, ODD_IN_PAD)
            hk = GLA_HEADS * GLA_K_DIM
            hv = GLA_HEADS * GLA_V_DIM
            s0 = jnp.zeros((b, hv, hk), F32)
            h0 = jnp.zeros((b, SUBLANES, LRU_WIDTH), F32)
            cw = lru_conv_w[o]
            cb = lru_conv_b[o].reshape(1, LRU_WIDTH)
            outs = []
            for dr in range(2):
                rev = dr == 1
                gw = jnp.zeros((LANES, hk), F32).at[dr * GLA_GATE_RANK:(dr + 1) * GLA_GATE_RANK].set(gla_gate_w[o, dr])
                gb = gla_gate_b[o, dr].reshape(1, hk)
                o_c, s_c = _gla_call(p_c, gw, gb, s0, tb=_tile_rows(lc, 256), rev=rev, name=f"gla_c{li}d{dr}")
                o_l, _ = _gla_call(p_l, gw, gb, s_c, tb=_tile_rows(l, 512), rev=rev, name=f"gla_l{li}d{dr}")
                wg = jnp.concatenate([_block_diag(lru_wa[o, dr]), _block_diag(lru_wx[o, dr])], axis=-1).astype(BF16)
                bg = jnp.concatenate([lru_ba[o, dr], lru_bx[o, dr]]).reshape(1, 2 * LRU_WIDTH)
                lam = lru_lam[o, dr].reshape(1, LRU_WIDTH)
                h_c, hT = _lru_call(p_c, cw, cb, wg, bg, lam, h0, tb=_tile_rows(lc, 256), rev=rev,
                                    name=f"lru_c{li}d{dr}")
                h_l, _ = _lru_call(p_l, cw, cb, wg, bg, lam, hT, tb=_tile_rows(l, 256), rev=rev,
                                   name=f"lru_l{li}d{dr}")
                outs.append((o_c, o_l, h_c, h_l))
            mix_l = (outs[0][1], outs[1][1], outs[0][3], outs[1][3], p_l, gla_norm_g[o])
            xl = _out_ffn_call(xl, mix_l, mod3, lat_row, norm_g[li, 1], w_out, wf_in, wf_out, fg,
                               kind=1, tm=tm_l, f_chunk=f_chunk, name=f"out_ffn_l{li}")
            if not last:
                flat = lambda a: a.reshape(1, b * lc, a.shape[-1])
                mix_c = (flat(outs[0][0]), flat(outs[1][0]), flat(outs[0][2]), flat(outs[1][2]),
                         flat(p_c), gla_norm_g[o])
                xc = _out_ffn_call(xc, mix_c, mod3, ctx_row, norm_g[li, 1], w_out, wf_in, wf_out, None,
                                   kind=1, tm=tm_c, f_chunk=f_chunk, name=f"out_ffn_c{li}")
    return xl
```

```python
import functools
import math

import jax
import jax.numpy as jnp
from jax import lax
from jax.experimental import pallas as pl
from jax.experimental.pallas import tpu as pltpu

F32 = jnp.float32
BF16 = jnp.bfloat16

NORM_EPS = 1e-6
ROPE_THETA = 10000.0
ROPE_DIM = 64
GRID_W = 64
MASK_VALUE = -1e30

DA_HEADS = 4
DA_QK_DIM = 64
DA_V_DIM = 128
WA_Q_HEADS = 8
WA_KV_HEADS = 2
WA_GROUP = WA_Q_HEADS // WA_KV_HEADS
WA_HEAD_DIM = 64
WINDOW = 128

GLA_HEADS = 4
GLA_K_DIM = 64
GLA_V_DIM = 128
GLA_GATE_RANK = 16
GLA_GATE_NORM = 16.0
LRU_WIDTH = 512
LRU_BLOCKS = 8
LRU_CONV = 4
LRU_C = 8.0

LANES = 128
SUBLANES = 8
VMEM_LIMIT_BYTES = 56 * 1024 * 1024

GLA_CHUNK = 64
GLA_SUB = 16
EVEN_IN = 2304
ODD_IN_PAD = 2688


def _params(*sem):
    return pltpu.CompilerParams(dimension_semantics=sem, vmem_limit_bytes=VMEM_LIMIT_BYTES)


def _resident(block_shape, index_map):
    return pl.BlockSpec(block_shape, index_map, pipeline_mode=pl.Buffered(1))


def _softplus(x):
    return jnp.maximum(x, 0.0) + jnp.log1p(jnp.exp(-jnp.abs(x)))


def _silu(x):
    return x * jax.nn.sigmoid(x)


def _gelu_tanh(x):
    return 0.5 * x * (1.0 + jnp.tanh(math.sqrt(2.0 / math.pi) * (x + 0.044715 * (x * x * x))))


def _dot_nt(a, b):
    return lax.dot_general(a, b, (((1,), (1,)), ((), ())), preferred_element_type=F32)


def _dot_tn(a, b):
    return lax.dot_general(a, b, (((0,), (0,)), ((), ())), preferred_element_type=F32)


def _ada_kernel(cond_ref, w_ref, b_ref, o_ref):
    s = _silu(cond_ref[...])
    o_ref[...] = jnp.dot(s, w_ref[...], preferred_element_type=F32,
                         precision=lax.Precision.HIGHEST) + b_ref[...]


def _ada_call(cond, ada_w, ada_b):
    depth, d, n = ada_w.shape
    tn = 1536
    return pl.pallas_call(
        _ada_kernel,
        grid=(depth, n // tn),
        in_specs=[pl.BlockSpec((SUBLANES, d), lambda l, j: (0, 0)),
                  pl.BlockSpec((None, d, tn), lambda l, j: (l, 0, j)),
                  pl.BlockSpec((None, 1, tn), lambda l, j: (l, 0, j))],
        out_specs=pl.BlockSpec((None, SUBLANES, tn), lambda l, j: (l, 0, j)),
        out_shape=jax.ShapeDtypeStruct((depth, SUBLANES, n), F32),
        compiler_params=_params("arbitrary", "arbitrary"),
        name="ada_mod",
    )(cond, ada_w, ada_b.reshape(depth, 1, n))


def _norm_mod(x, g, shift, scale):
    ms = jnp.mean(x * x, axis=-1, keepdims=True)
    return (x * lax.rsqrt(ms + NORM_EPS) * g) * (1.0 + scale) + shift


def _in_proj_kernel(*refs, rope_groups, n_chunk):
    if rope_groups:
        x_ref, mod_ref, g_ref, w_ref, cs_ref, o_ref = refs
    else:
        x_ref, mod_ref, g_ref, w_ref, o_ref = refs
    x = x_ref[...]
    d = x.shape[-1]
    mod = mod_ref[...]
    hb = _norm_mod(x, g_ref[...], mod[:, 0:d], mod[:, d:2 * d]).astype(BF16)
    n = w_ref.shape[1]
    if rope_groups:
        cos = cs_ref[:, 0:LANES]
        sin = cs_ref[:, LANES:2 * LANES]
        lane = lax.broadcasted_iota(jnp.int32, (x.shape[0], LANES), 1)
        first_half = (lane % 32) < 16
    for n0 in range(0, n, n_chunk):
        nw = min(n_chunk, n - n0)
        acc = jnp.dot(hb, w_ref[:, n0:n0 + nw], preferred_element_type=F32)
        if not rope_groups:
            o_ref[:, n0:n0 + nw] = acc.astype(o_ref.dtype)
            continue
        for j in range(nw // LANES):
            grp = acc[:, j * LANES:(j + 1) * LANES]
            if (n0 // LANES + j) in rope_groups:
                partner = jnp.where(first_half, pltpu.roll(grp, LANES - 16, 1), pltpu.roll(grp, 16, 1))
                grp = grp * cos + partner * sin
            c0 = n0 + j * LANES
            o_ref[:, c0:c0 + LANES] = grp.astype(o_ref.dtype)


def _in_proj_call(x, mod3, mod_row, g, w, cs, *, tm, n_chunk, rope_groups, out_dtype, name):
    b, t, d = x.shape
    n = w.shape[1]
    in_specs = [pl.BlockSpec((None, tm, d), lambda bi, i: (bi, i, 0)),
                pl.BlockSpec((None, 1, mod3.shape[-1]), lambda bi, i: (mod_row(bi), 0, 0)),
                pl.BlockSpec((1, d), lambda bi, i: (0, 0)),
                _resident((d, n), lambda bi, i: (0, 0))]
    args = [x, mod3, g.reshape(1, d), w]
    if rope_groups:
        in_specs.append(pl.BlockSpec((tm, 2 * LANES), lambda bi, i: (i, 0)))
        args.append(cs)
    return pl.pallas_call(
        functools.partial(_in_proj_kernel, rope_groups=rope_groups, n_chunk=n_chunk),
        grid=(b, t // tm),
        in_specs=in_specs,
        out_specs=pl.BlockSpec((None, tm, n), lambda bi, i: (bi, i, 0)),
        out_shape=jax.ShapeDtypeStruct((b, t, n), out_dtype),
        compiler_params=_params("parallel", "parallel"),
        name=name,
    )(*args)


def _diff_attn_kernel(*refs, tk, has_lat, lam_init, n_row_blocks):
    if has_lat:
        lam_ref, q_ref, kc_ref, vc_ref, kl_ref, vl_ref, o_ref, kx, vx, s_a, s_b = refs
    else:
        lam_ref, q_ref, kc_ref, vc_ref, o_ref, kx, vx, s_a, s_b = refs
    lc = kc_ref.shape[0]
    t = kx.shape[0]

    @pl.when(pl.program_id(2) == 0)
    def _():
        kx[0:lc, :] = kc_ref[...]
        vx[0:lc, 0:DA_V_DIM] = vc_ref[...]
        if has_lat:
            kx[lc:t, :] = kl_ref[...]
            vx[lc:t, 0:DA_V_DIM] = vl_ref[...]
        lane_t = lax.broadcasted_iota(jnp.int32, (t, LANES), 1)
        vx[:, DA_V_DIM:DA_V_DIM + LANES] = jnp.where(lane_t == 0, 1.0, 0.0).astype(BF16)

    q = q_ref[...]
    tq = q.shape[0]
    lane = lax.broadcasted_iota(jnp.int32, q.shape, 1)
    zero = jnp.zeros_like(q)
    qmaps = (jnp.where(lane < DA_QK_DIM, q, zero), jnp.where(lane >= DA_QK_DIM, q, zero))
    rows = tq // n_row_blocks
    qm = [qmap[r * rows:(r + 1) * rows] for r in range(n_row_blocks) for qmap in qmaps]

    def chunk(j):
        return pl.ds(j * tk if isinstance(j, int) else pl.multiple_of(j * tk, tk), tk)

    def scores_into(buf, j):
        k = kx[chunk(j), :]
        for ci in range(len(qm)):
            buf[ci] = _dot_nt(qm[ci], k)

    def consume(buf, j, carry):
        v = vx[chunk(j), :]
        out = []
        for ci in range(len(qm)):
            m, acc = carry[ci]
            s = buf[ci]
            m_new = jnp.maximum(m, jnp.max(s, axis=-1, keepdims=True))
            p = jnp.exp2(s - m_new)
            acc = jnp.exp2(m - m_new) * acc + jnp.dot(p.astype(BF16), v, preferred_element_type=F32)
            out.append((m_new, acc))
        return tuple(out)

    n = t // tk
    n_pairs = (n - 1) // 2
    buf_a, buf_b = s_a, s_b
    scores_into(buf_a, 0)

    def body(i, carry):
        scores_into(buf_b, 2 * i + 1)
        carry = consume(buf_a, 2 * i, carry)
        scores_into(buf_a, 2 * i + 2)
        return consume(buf_b, 2 * i + 1, carry)

    carry = tuple((jnp.full((rows, 1), MASK_VALUE, F32), jnp.zeros((rows, DA_V_DIM + LANES), F32)) for _ in qm)
    for i in range(n_pairs):
        carry = body(i, carry)
    if n % 2 == 0:
        scores_into(buf_b, n - 1)
        carry = consume(buf_a, n - 2, carry)
        carry = consume(buf_b, n - 1, carry)
    else:
        carry = consume(buf_a, n - 1, carry)
    o = [acc[:, 0:DA_V_DIM] / acc[:, DA_V_DIM:DA_V_DIM + 1] for _, acc in carry]
    lv = lam_ref[...]
    lam = (jnp.exp(jnp.sum(lv[0:1] * lv[1:2], axis=-1, keepdims=True))
           - jnp.exp(jnp.sum(lv[2:3] * lv[3:4], axis=-1, keepdims=True)) + lam_init)
    for r in range(n_row_blocks):
        w = o[2 * r] - lam * o[2 * r + 1]
        y = w * lax.rsqrt(jnp.mean(w * w, axis=-1, keepdims=True) + NORM_EPS) * (1.0 - lam_init)
        o_ref[r * rows:(r + 1) * rows, :] = y.astype(o_ref.dtype)


def _diff_attn_call(lam_vec, pq, pc, pl_kv, *, tq, n_row_blocks, tk_cap, lam_init, name):
    b, t, _ = pq.shape
    lc = pc.shape[1]
    kcol, vcol = DA_HEADS, 2 * DA_HEADS
    in_specs = [pl.BlockSpec((4, DA_QK_DIM), lambda bi, h, i: (0, 0)),
                pl.BlockSpec((None, tq, LANES), lambda bi, h, i: (bi, i, h)),
                pl.BlockSpec((None, lc, LANES), lambda bi, h, i: (bi, 0, kcol + h)),
                pl.BlockSpec((None, lc, LANES), lambda bi, h, i: (bi, 0, vcol + h))]
    args = [lam_vec, pq, pc, pc]
    n_keys = lc
    if pl_kv is not None:
        l = pl_kv.shape[1]
        n_keys += l
        in_specs += [pl.BlockSpec((None, l, LANES), lambda bi, h, i: (bi, 0, kcol + h)),
                     pl.BlockSpec((None, l, LANES), lambda bi, h, i: (bi, 0, vcol + h))]
        args += [pl_kv, pl_kv]
    tk = max(c for c in range(LANES, tk_cap + 1, LANES) if n_keys % c == 0)
    return pl.pallas_call(
        functools.partial(_diff_attn_kernel, tk=tk, has_lat=pl_kv is not None, lam_init=lam_init,
                          n_row_blocks=n_row_blocks),
        grid=(b, DA_HEADS, t // tq),
        in_specs=in_specs,
        out_specs=pl.BlockSpec((None, tq, LANES), lambda bi, h, i: (bi, i, h)),
        out_shape=jax.ShapeDtypeStruct((b, t, DA_HEADS * DA_V_DIM), BF16),
        scratch_shapes=[pltpu.VMEM((n_keys, LANES), BF16), pltpu.VMEM((n_keys, DA_V_DIM + LANES), BF16),
                        pltpu.VMEM((2 * n_row_blocks, tq // n_row_blocks, tk), F32),
                        pltpu.VMEM((2 * n_row_blocks, tq // n_row_blocks, tk), F32)],
        compiler_params=_params("arbitrary", "arbitrary", "arbitrary"),
        name=name,
    )(*args)


def _win_attn_kernel(*refs, tq, span, has_win):
    if has_win:
        sink_ref, q_ref, kc_ref, vc_ref, kl_ref, vl_ref, o_ref = refs
    else:
        sink_ref, q_ref, kc_ref, vc_ref, o_ref = refs
    q = q_ref[...]
    kc = kc_ref[...]
    vc = vc_ref[...]
    if has_win:
        l = kl_ref.shape[0]
        q0 = pl.program_id(1) * tq
        start = pl.multiple_of(jnp.clip(q0 - WINDOW, 0, l - span), WINDOW)
        kw = kl_ref[pl.ds(start, span), :]
        vw = vl_ref[pl.ds(start, span), :]
        qpos = q0 + lax.broadcasted_iota(jnp.int32, (tq, span), 0)
        kpos = start + lax.broadcasted_iota(jnp.int32, (tq, span), 1)
        valid = jnp.abs(kpos - qpos) <= WINDOW
    for hk in range(WA_KV_HEADS):
        ks = slice(hk * WA_HEAD_DIM, (hk + 1) * WA_HEAD_DIM)
        kc_h, vc_h = kc[:, ks], vc[:, ks]
        if has_win:
            kw_h, vw_h = kw[:, ks], vw[:, ks]
        for g in range(WA_GROUP):
            h = hk * WA_GROUP + g
            hs = slice(h * WA_HEAD_DIM, (h + 1) * WA_HEAD_DIM)
            q_h = q[:, hs]
            sink = sink_ref[h] * math.log2(math.e)
            s_c = _dot_nt(q_h, kc_h)
            m = jnp.maximum(jnp.max(s_c, axis=-1, keepdims=True), sink)
            if has_win:
                s_w = jnp.where(valid, _dot_nt(q_h, kw_h), MASK_VALUE)
                m = jnp.maximum(m, jnp.max(s_w, axis=-1, keepdims=True))
            p_c = jnp.exp2(s_c - m)
            den = jnp.sum(p_c, axis=-1, keepdims=True) + jnp.exp2(sink - m)
            num = jnp.dot(p_c.astype(BF16), vc_h, preferred_element_type=F32)
            if has_win:
                p_w = jnp.exp2(s_w - m)
                den = den + jnp.sum(p_w, axis=-1, keepdims=True)
                num = num + jnp.dot(p_w.astype(BF16), vw_h, preferred_element_type=F32)
            o_ref[:, hs] = (num / den).astype(o_ref.dtype)


def _win_attn_call(sink, pq, pc, pl_kv, *, tq, name):
    b, t, _ = pq.shape
    lc = pc.shape[1]
    qcol = 3
    kcol, vcol = 16, 17
    in_specs = [pl.BlockSpec(memory_space=pltpu.SMEM),
                pl.BlockSpec((None, tq, 4 * LANES), lambda bi, i: (bi, i, qcol)),
                pl.BlockSpec((None, lc, LANES), lambda bi, i: (bi, 0, kcol)),
                pl.BlockSpec((None, lc, LANES), lambda bi, i: (bi, 0, vcol))]
    args = [sink, pq, pc, pc]
    has_win = pl_kv is not None
    if has_win:
        l = pl_kv.shape[1]
        in_specs += [pl.BlockSpec((None, l, LANES), lambda bi, i: (bi, 0, kcol)),
                     pl.BlockSpec((None, l, LANES), lambda bi, i: (bi, 0, vcol))]
        args += [pl_kv, pl_kv]
    return pl.pallas_call(
        functools.partial(_win_attn_kernel, tq=tq, span=tq + 2 * WINDOW, has_win=has_win),
        grid=(b, t // tq),
        in_specs=in_specs,
        out_specs=pl.BlockSpec((None, tq, 4 * LANES), lambda bi, i: (bi, i, 0)),
        out_shape=jax.ShapeDtypeStruct((b, t, WA_Q_HEADS * WA_HEAD_DIM), BF16),
        compiler_params=_params("parallel", "arbitrary"),
        name=name,
    )(*args)


def _gla_chunk(q, k, v, g, state, rev):
    c, w = q.shape
    nsub = c // GLA_SUB
    row = lax.broadcasted_iota(jnp.int32, (c, w), 0)
    rb = row % GLA_SUB
    bl = g
    s = 1
    while s < GLA_SUB:
        if rev:
            bl = bl + jnp.where(rb < GLA_SUB - s, pltpu.roll(bl, c - s, 0), 0.0)
        else:
            bl = bl + jnp.where(rb >= s, pltpu.roll(bl, s, 0), 0.0)
        s *= 2
    order = list(range(nsub))[::-1] if rev else list(range(nsub))
    rank = {blk: n for n, blk in enumerate(order)}

    def rows(a, i):
        return a[i * GLA_SUB:(i + 1) * GLA_SUB]

    tot = {}
    for i in range(nsub):
        r = i * GLA_SUB if rev else i * GLA_SUB + GLA_SUB - 1
        tot[i] = bl[r:r + 1]
    pre = {}
    run = jnp.zeros((1, w), F32)
    for i in order:
        pre[i] = run
        run = run + tot[i]
    total = run

    def cat(fn):
        return jnp.concatenate([fn(i) for i in range(nsub)], axis=0)

    zeros = jnp.zeros((GLA_SUB, w), F32)
    b_full = cat(lambda i: rows(bl, i) + pre[i])
    qe = q * jnp.exp(b_full)
    kdec = k * jnp.exp(total - b_full)
    qd = q * jnp.exp(bl)
    kd = k * jnp.exp(-bl)
    ko = cat(lambda i: rows(k, i) * jnp.exp(tot[i] - rows(bl, i)))

    def stack_keys(x):
        r = lax.broadcasted_iota(jnp.int32, (GLA_HEADS * c, w), 0) // c
        ln = lax.broadcasted_iota(jnp.int32, (GLA_HEADS * c, w), 1) // GLA_K_DIM
        return jnp.where(r == ln, jnp.concatenate([x] * GLA_HEADS, axis=0), 0.0).astype(BF16)

    qi = lax.broadcasted_iota(jnp.int32, (c, GLA_HEADS * c), 0)
    kj = lax.broadcasted_iota(jnp.int32, (c, GLA_HEADS * c), 1) % c
    same = (qi // GLA_SUB) == (kj // GLA_SUB)
    tri = (qi <= kj) if rev else (qi >= kj)
    a = jnp.where(same & tri, _dot_nt(qd.astype(BF16), stack_keys(kd)), 0.0)
    for j in order[:-1]:
        off = pre[j] + tot[j]
        qo = cat(lambda i: rows(q, i) * jnp.exp(rows(b_full, i) - off) if rank[i] > rank[j] else zeros)
        koj = cat(lambda i: rows(ko, i) if i == j else zeros)
        a = a + _dot_nt(qo.astype(BF16), stack_keys(koj))

    vr = lax.broadcasted_iota(jnp.int32, (GLA_HEADS * c, v.shape[1]), 0) // c
    vl = lax.broadcasted_iota(jnp.int32, (GLA_HEADS * c, v.shape[1]), 1) // GLA_V_DIM
    vst = jnp.where(vr == vl, jnp.concatenate([v] * GLA_HEADS, axis=0), 0.0).astype(BF16)
    o = _dot_nt(qe.astype(BF16), state.astype(BF16)) + jnp.dot(a.astype(BF16), vst, preferred_element_type=F32)

    sr = lax.broadcasted_iota(jnp.int32, state.shape, 0) // GLA_V_DIM
    sl = lax.broadcasted_iota(jnp.int32, state.shape, 1) // GLA_K_DIM
    upd = _dot_tn(v.astype(BF16), kdec.astype(BF16))
    new_state = state * jnp.exp(total) + jnp.where(sr == sl, upd, 0.0)
    return o, new_state


def _gla_kernel(q_ref, k_ref, v_ref, lr_ref, gw_ref, gb_ref, s0_ref, o_ref, sT_ref, s_scr, *, rev, nc):
    i = pl.program_id(1)
    nbatch = q_ref.shape[0]

    @pl.when(i == 0)
    def _():
        s_scr[...] = s0_ref[...]

    gates, states = [], []
    for bb in range(nbatch):
        logit = jnp.dot(lr_ref[bb], gw_ref[...], preferred_element_type=F32) + gb_ref[...]
        gates.append((jnp.minimum(logit, 0.0) - jnp.log1p(jnp.exp(-jnp.abs(logit)))) * (1.0 / GLA_GATE_NORM))
        states.append(s_scr[bb])
    for n in range(nc):
        c0 = ((nc - 1 - n) if rev else n) * GLA_CHUNK
        sl = slice(c0, c0 + GLA_CHUNK)
        for bb in range(nbatch):
            o, states[bb] = _gla_chunk(q_ref[bb, sl, :], k_ref[bb, sl, :], v_ref[bb, sl, :], gates[bb][sl],
                                       states[bb], rev)
            o_ref[bb, sl, :] = o
    for bb in range(nbatch):
        s_scr[bb] = states[bb]

    @pl.when(i == pl.num_programs(1) - 1)
    def _():
        sT_ref[...] = s_scr[...]


def _gla_call(p, gw, gb, s0, *, tb, rev, name):
    b, t, _ = p.shape
    nb = t // tb
    hk = GLA_HEADS * GLA_K_DIM
    hv = GLA_HEADS * GLA_V_DIM
    bb = 2 if b % 2 == 0 else 1

    def tblk(i):
        return (nb - 1 - i) if rev else i

    return pl.pallas_call(
        functools.partial(_gla_kernel, rev=rev, nc=tb // GLA_CHUNK),
        grid=(b // bb, nb),
        in_specs=[pl.BlockSpec((bb, tb, hk), lambda bi, i: (bi, tblk(i), 0)),
                  pl.BlockSpec((bb, tb, hk), lambda bi, i: (bi, tblk(i), 1)),
                  pl.BlockSpec((bb, tb, hv), lambda bi, i: (bi, tblk(i), 1)),
                  pl.BlockSpec((bb, tb, LANES), lambda bi, i: (bi, tblk(i), 2560 // LANES)),
                  pl.BlockSpec((LANES, hk), lambda bi, i: (0, 0)),
                  pl.BlockSpec((1, hk), lambda bi, i: (0, 0)),
                  pl.BlockSpec((bb, hv, hk), lambda bi, i: (bi, 0, 0))],
        out_specs=[pl.BlockSpec((bb, tb, hv), lambda bi, i: (bi, tblk(i), 0)),
                   pl.BlockSpec((bb, hv, hk), lambda bi, i: (bi, 0, 0))],
        out_shape=[jax.ShapeDtypeStruct((b, t, hv), F32), jax.ShapeDtypeStruct((b, hv, hk), F32)],
        scratch_shapes=[pltpu.VMEM((bb, hv, hk), F32)],
        compiler_params=_params("parallel", "arbitrary"),
        name=name,
    )(p, p, p, p, gw, gb, s0)


def _lru_kernel(zx_ref, zp_ref, zn_ref, cw_ref, cb_ref, wg_ref, bg_ref, lam_ref, h0_ref,
                h_ref, hT_ref, carry_scr, *, rev, nb):
    i = pl.program_id(1)
    blk = (nb - 1 - i) if rev else i

    @pl.when(i == 0)
    def _():
        carry_scr[...] = h0_ref[...]

    x = zx_ref[...]
    tb, w = x.shape
    row = lax.broadcasted_iota(jnp.int32, (tb, w), 0)
    pm = jnp.where(blk > 0, zp_ref[SUBLANES - 1:SUBLANES, :], 0.0)
    n0 = jnp.where(blk < nb - 1, zn_ref[0:1, :], 0.0)
    n1 = jnp.where(blk < nb - 1, zn_ref[1:2, :], 0.0)
    xm1 = jnp.where(row == 0, pm, pltpu.roll(x, 1, 0))
    xp1 = jnp.where(row == tb - 1, n0, pltpu.roll(x, tb - 1, 0))
    xp2 = jnp.where(row == tb - 1, n1, jnp.where(row == tb - 2, n0, pltpu.roll(x, tb - 2, 0)))
    cw = cw_ref[...]
    xr = xm1 * cw[0:1] + x * cw[1:2] + xp1 * cw[2:3] + xp2 * cw[3:4] + cb_ref[...]

    gm = jnp.dot(xr.astype(BF16), wg_ref[...], preferred_element_type=F32) + bg_ref[...]
    r = 0.5 * jnp.tanh(0.5 * gm[:, 0:w]) + 0.5
    ig = 0.5 * jnp.tanh(0.5 * gm[:, w:2 * w]) + 0.5
    log_a = (-LRU_C * r) * _softplus(-lam_ref[...])
    a = jnp.exp(log_a)
    u = jnp.sqrt(-jnp.tanh(log_a) * (a * a + 1.0)) * (ig * xr)

    sub = lax.broadcasted_iota(jnp.int32, (SUBLANES, w), 0)
    h_prev = carry_scr[0:1, :]
    ng = tb // SUBLANES
    for gi in (range(ng - 1, -1, -1) if rev else range(ng)):
        rs = slice(gi * SUBLANES, (gi + 1) * SUBLANES)
        ag, ug = a[rs], u[rs]
        s = 1
        while s < SUBLANES:
            if rev:
                keep = sub < SUBLANES - s
                a_sh = jnp.where(keep, pltpu.roll(ag, SUBLANES - s, 0), 1.0)
                u_sh = jnp.where(keep, pltpu.roll(ug, SUBLANES - s, 0), 0.0)
            else:
                keep = sub >= s
                a_sh = jnp.where(keep, pltpu.roll(ag, s, 0), 1.0)
                u_sh = jnp.where(keep, pltpu.roll(ug, s, 0), 0.0)
            ug = ag * u_sh + ug
            ag = ag * a_sh
            s *= 2
        hg = ug + ag * h_prev
        h_ref[rs, :] = hg
        h_prev = hg[0:1] if rev else hg[SUBLANES - 1:SUBLANES]
    carry_scr[...] = jnp.broadcast_to(h_prev, carry_scr.shape)

    @pl.when(i == nb - 1)
    def _():
        hT_ref[...] = carry_scr[...]


def _lru_call(p, cw, cb, wg, bg, lam, h0, *, tb, rev, name):
    b, t, _ = p.shape
    nb = t // tb
    w = LRU_WIDTH
    zcol = 2048 // w
    per = tb // SUBLANES

    def tblk(i):
        return (nb - 1 - i) if rev else i

    return pl.pallas_call(
        functools.partial(_lru_kernel, rev=rev, nb=nb),
        grid=(b, nb),
        in_specs=[pl.BlockSpec((None, tb, w), lambda bi, i: (bi, tblk(i), zcol)),
                  pl.BlockSpec((None, SUBLANES, w), lambda bi, i: (bi, jnp.maximum(tblk(i) * per - 1, 0), zcol)),
                  pl.BlockSpec((None, SUBLANES, w),
                               lambda bi, i: (bi, jnp.minimum((tblk(i) + 1) * per, nb * per - 1), zcol)),
                  pl.BlockSpec((LRU_CONV, w), lambda bi, i: (0, 0)),
                  pl.BlockSpec((1, w), lambda bi, i: (0, 0)),
                  pl.BlockSpec((w, 2 * w), lambda bi, i: (0, 0)),
                  pl.BlockSpec((1, 2 * w), lambda bi, i: (0, 0)),
                  pl.BlockSpec((1, w), lambda bi, i: (0, 0)),
                  pl.BlockSpec((None, SUBLANES, w), lambda bi, i: (bi, 0, 0))],
        out_specs=[pl.BlockSpec((None, tb, w), lambda bi, i: (bi, tblk(i), 0)),
                   pl.BlockSpec((None, SUBLANES, w), lambda bi, i: (bi, 0, 0))],
        out_shape=[jax.ShapeDtypeStruct((b, t, w), F32), jax.ShapeDtypeStruct((b, SUBLANES, w), F32)],
        scratch_shapes=[pltpu.VMEM((SUBLANES, w), F32)],
        compiler_params=_params("parallel", "arbitrary"),
        name=name,
    )(p, p, p, cw, cb, wg, bg, lam, h0)


def _out_ffn_kernel(*refs, kind, final, f_chunk):
    if kind == 0:
        x_ref, ma_ref, mb_ref = refs[0:3]
        rest = refs[3:]
    else:
        x_ref, of_ref, ob_ref, hf_ref, hb_ref, gg_ref, zg_ref, glag_ref = refs[0:8]
        rest = refs[8:]
    if final:
        mod_ref, g2_ref, wo_ref, wi_ref, wf_ref, fg_ref, o_ref = rest
    else:
        mod_ref, g2_ref, wo_ref, wi_ref, wf_ref, o_ref = rest
    x = x_ref[...]
    d = x.shape[-1]
    if kind == 0:
        mix_a = ma_ref[...]
        mix_b = mb_ref[...]
    else:
        og = of_ref[...] + ob_ref[...]
        gg = gg_ref[...]
        parts = []
        for h in range(GLA_HEADS):
            hs = slice(h * GLA_V_DIM, (h + 1) * GLA_V_DIM)
            seg = og[:, hs]
            nrm = seg * lax.rsqrt(jnp.mean(seg * seg, axis=-1, keepdims=True) + NORM_EPS) * glag_ref[...]
            parts.append(nrm * _silu(gg[:, hs]))
        mix_a = jnp.concatenate(parts, axis=-1).astype(BF16)
        mix_b = ((hf_ref[...] + hb_ref[...]) * _gelu_tanh(zg_ref[...])).astype(BF16)
    wa = mix_a.shape[-1]
    y = (jnp.dot(mix_a, wo_ref[0:wa, :], preferred_element_type=F32)
         + jnp.dot(mix_b, wo_ref[wa:, :], preferred_element_type=F32))
    mod = mod_ref[...]
    x1 = x + mod[:, 2 * d:3 * d] * y
    h2 = _norm_mod(x1, g2_ref[...], mod[:, 3 * d:4 * d], mod[:, 4 * d:5 * d]).astype(BF16)
    f = wf_ref.shape[0]
    acc = None
    for f0 in range(0, f, f_chunk):
        gate = jnp.dot(h2, wi_ref[:, f0:f0 + f_chunk], preferred_element_type=F32)
        up = jnp.dot(h2, wi_ref[:, f + f0:f + f0 + f_chunk], preferred_element_type=F32)
        act = (_silu(gate) * up).astype(BF16)
        part = jnp.dot(act, wf_ref[f0:f0 + f_chunk, :], preferred_element_type=F32)
        acc = part if acc is None else acc + part
    x2 = x1 + mod[:, 5 * d:6 * d] * acc
    if final:
        ms = jnp.mean(x2 * x2, axis=-1, keepdims=True)
        x2 = x2 * lax.rsqrt(ms + NORM_EPS) * fg_ref[...]
    o_ref[...] = x2


def _out_ffn_call(x, mix, mod3, mod_row, g2, wo, wi, wf, final_g, *, kind, tm, f_chunk, name):
    b, t, d = x.shape
    f = wf.shape[0]

    def tile(width, col=0):
        return pl.BlockSpec((None, tm, width), lambda bi, i: (bi, i, col))

    in_specs = [tile(d)]
    args = [x]
    if kind == 0:
        ma, mb = mix
        in_specs += [tile(ma.shape[-1]), tile(mb.shape[-1])]
        args += [ma, mb]
    else:
        o_f, o_b, h_f, h_b, p, gla_g = mix
        hv = GLA_HEADS * GLA_V_DIM
        in_specs += [tile(hv), tile(hv), tile(LRU_WIDTH), tile(LRU_WIDTH),
                     tile(hv, 1024 // hv), tile(LRU_WIDTH, 1536 // LRU_WIDTH),
                     pl.BlockSpec((1, GLA_V_DIM), lambda bi, i: (0, 0))]
        args += [o_f, o_b, h_f, h_b, p, p, gla_g.reshape(1, GLA_V_DIM)]
    in_specs += [pl.BlockSpec((None, 1, mod3.shape[-1]), lambda bi, i: (mod_row(bi), 0, 0)),
                 pl.BlockSpec((1, d), lambda bi, i: (0, 0)),
                 _resident(wo.shape, lambda bi, i: (0, 0)),
                 _resident(wi.shape, lambda bi, i: (0, 0)),
                 _resident(wf.shape, lambda bi, i: (0, 0))]
    args += [mod3, g2.reshape(1, d), wo, wi, wf]
    final = final_g is not None
    if final:
        in_specs.append(pl.BlockSpec((1, d), lambda bi, i: (0, 0)))
        args.append(final_g.reshape(1, d))
    return pl.pallas_call(
        functools.partial(_out_ffn_kernel, kind=kind, final=final, f_chunk=f_chunk),
        grid=(b, t // tm),
        in_specs=in_specs,
        out_specs=pl.BlockSpec((None, tm, d), lambda bi, i: (bi, i, 0)),
        out_shape=jax.ShapeDtypeStruct((b, t, d), F32),
        compiler_params=_params("parallel", "parallel"),
        name=name,
    )(*args)


def _rope_table(n_tokens):
    n_rows = n_tokens // GRID_W
    row = jnp.broadcast_to(jnp.arange(n_rows)[:, None], (n_rows, GRID_W)).reshape(-1)
    col = jnp.broadcast_to(jnp.arange(GRID_W)[None, :], (n_rows, GRID_W)).reshape(-1)
    axis_dim = ROPE_DIM // 2
    inv_freq = ROPE_THETA ** (-jnp.arange(0, axis_dim, 2, dtype=F32) / axis_dim)
    ang_r = row.astype(F32)[:, None] * inv_freq
    ang_c = col.astype(F32)[:, None] * inv_freq
    cr, sr, cc, sc = jnp.cos(ang_r), jnp.sin(ang_r), jnp.cos(ang_c), jnp.sin(ang_c)
    cos = jnp.concatenate([cr, cr, cc, cc], axis=-1)
    sin = jnp.concatenate([-sr, sr, -sc, sc], axis=-1)
    reps = LANES // ROPE_DIM
    return jnp.concatenate([jnp.tile(cos, (1, reps)), jnp.tile(sin, (1, reps))], axis=-1)


def _even_w_in(w):
    qa, ka, va, qb, kb, vb = jnp.split(w, (512, 1024, 1536, 2048, 2176), axis=-1)
    qa_scale = DA_QK_DIM ** -0.5 * math.log2(math.e)
    qb_scale = WA_HEAD_DIM ** -0.5 * math.log2(math.e)
    return jnp.concatenate([qa * qa_scale, ka, va, qb * qb_scale, kb, vb], axis=-1).astype(BF16)


def _odd_w_in(w):
    q, k, v, g, lr, zg, zx = jnp.split(w, (256, 512, 1024, 1536, 1568, 2080), axis=-1)
    pad = jnp.zeros((w.shape[0], LANES - lr.shape[1]), w.dtype)
    return jnp.concatenate([q * GLA_K_DIM ** -0.5, k, v, g, zg, zx, lr, pad], axis=-1).astype(BF16)


def _block_diag(w):
    n, c, d = w.shape
    eye = jnp.eye(n, dtype=w.dtype)
    return (eye[:, None, :, None] * w[:, :, None, :]).reshape(n * c, n * d)


def _tile_rows(n, cap):
    t = min(n, cap)
    while n % t:
        t //= 2
    return t


def kernel(x, c, ctx, c_ctx, ada_w, ada_b, norm_g, even_w_in, even_w_out, diff_lam, win_sink, odd_w_in, odd_w_out, gla_gate_w, gla_gate_b, gla_norm_g, lru_conv_w, lru_conv_b, lru_wa, lru_ba, lru_wx, lru_bx, lru_lam, ffn_w_in, ffn_w_out, final_g):
    b, l, d = x.shape
    lc = ctx.shape[1]
    depth = ada_w.shape[0]
    assert b < SUBLANES and d == 1024

    cond = jnp.zeros((SUBLANES, d), F32).at[0:b].set(c).at[b].set(c_ctx)
    mod = _ada_call(cond, ada_w, ada_b)
    cs = _rope_table(l)

    xl = x
    xc = ctx.reshape(1, b * lc, d)
    tm_l = _tile_rows(l, 512)
    tm_c = _tile_rows(b * lc, 512)
    lat_row = lambda bi: bi
    ctx_row = lambda bi: b

    for li in range(depth):
        last = li == depth - 1
        mod3 = mod[li].reshape(SUBLANES, 1, 6 * d)
        wf_in = ffn_w_in[li].astype(BF16)
        wf_out = ffn_w_out[li].astype(BF16)
        fg = final_g if last else None
        f_chunk = wf_out.shape[0] // 2
        if li % 2 == 0:
            e = li // 2
            lam_init = 0.8 - 0.6 * math.exp(-0.3 * li)
            w_in = _even_w_in(even_w_in[e])
            w_out = even_w_out[e].astype(BF16)
            rope_groups = frozenset(range(0, 8)) | frozenset(range(12, 17))
            p_l = _in_proj_call(xl, mod3, lat_row, norm_g[li, 0], w_in, cs, tm=tm_l, n_chunk=768,
                                rope_groups=rope_groups, out_dtype=BF16, name=f"in_proj_l{li}")
            p_c = _in_proj_call(xc, mod3, ctx_row, norm_g[li, 0], w_in, None, tm=tm_c, n_chunk=768,
                                rope_groups=None, out_dtype=BF16, name=f"in_proj_c{li}")
            p_c = p_c.reshape(b, lc, EVEN_IN)
            tq = _tile_rows(l, 256)
            a_l = _diff_attn_call(diff_lam[e], p_l, p_c, p_l, tq=_tile_rows(l, 1024), n_row_blocks=4, tk_cap=768,
                                  lam_init=lam_init, name=f"diff_attn_l{li}")
            b_l = _win_attn_call(win_sink[e], p_l, p_c, p_l, tq=tq, name=f"win_attn_l{li}")
            xl = _out_ffn_call(xl, (a_l, b_l), mod3, lat_row, norm_g[li, 1], w_out, wf_in, wf_out, fg,
                               kind=0, tm=tm_l, f_chunk=f_chunk, name=f"out_ffn_l{li}")
            if not last:
                a_c = _diff_attn_call(diff_lam[e], p_c, p_c, None, tq=lc, n_row_blocks=2, tk_cap=768,
                                      lam_init=lam_init, name=f"diff_attn_c{li}")
                b_c = _win_attn_call(win_sink[e], p_c, p_c, None, tq=lc, name=f"win_attn_c{li}")
                mix_c = (a_c.reshape(1, b * lc, -1), b_c.reshape(1, b * lc, -1))
                xc = _out_ffn_call(xc, mix_c, mod3, ctx_row, norm_g[li, 1], w_out, wf_in, wf_out, None,
                                   kind=0, tm=tm_c, f_chunk=f_chunk, name=f"out_ffn_c{li}")
        else:
            o = li // 2
            w_in = _odd_w_in(odd_w_in[o])
            w_out = odd_w_out[o].astype(BF16)
            p_l = _in_proj_call(xl, mod3, lat_row, norm_g[li, 0], w_in, None, tm=tm_l, n_chunk=768,
                                rope_groups=None, out_dtype=F32, name=f"in_proj_l{li}")
            p_c = _in_proj_call(xc, mod3, ctx_row, norm_g[li, 0], w_in, None, tm=tm_c, n_chunk=768,
                                rope_groups=None, out_dtype=F32, name=f"in_proj_c{li}")
            p_c = p_c.reshape(b, lc, ODD_IN_PAD)
            hk = GLA_HEADS * GLA_K_DIM
            hv = GLA_HEADS * GLA_V_DIM
            s0 = jnp.zeros((b, hv, hk), F32)
            h0 = jnp.zeros((b, SUBLANES, LRU_WIDTH), F32)
            cw = lru_conv_w[o]
            cb = lru_conv_b[o].reshape(1, LRU_WIDTH)
            outs = []
            for dr in range(2):
                rev = dr == 1
                gw = jnp.zeros((LANES, hk), F32).at[dr * GLA_GATE_RANK:(dr + 1) * GLA_GATE_RANK].set(gla_gate_w[o, dr])
                gb = gla_gate_b[o, dr].reshape(1, hk)
                o_c, s_c = _gla_call(p_c, gw, gb, s0, tb=_tile_rows(lc, 256), rev=rev, name=f"gla_c{li}d{dr}")
                o_l, _ = _gla_call(p_l, gw, gb, s_c, tb=_tile_rows(l, 512), rev=rev, name=f"gla_l{li}d{dr}")
                wg = jnp.concatenate([_block_diag(lru_wa[o, dr]), _block_diag(lru_wx[o, dr])], axis=-1).astype(BF16)
                bg = jnp.concatenate([lru_ba[o, dr], lru_bx[o, dr]]).reshape(1, 2 * LRU_WIDTH)
                lam = lru_lam[o, dr].reshape(1, LRU_WIDTH)
                h_c, hT = _lru_call(p_c, cw, cb, wg, bg, lam, h0, tb=_tile_rows(lc, 256), rev=rev,
                                    name=f"lru_c{li}d{dr}")
                h_l, _ = _lru_call(p_l, cw, cb, wg, bg, lam, hT, tb=_tile_rows(l, 512), rev=rev,
                                   name=f"lru_l{li}d{dr}")
                outs.append((o_c, o_l, h_c, h_l))
            mix_l = (outs[0][1], outs[1][1], outs[0][3], outs[1][3], p_l, gla_norm_g[o])
            xl = _out_ffn_call(xl, mix_l, mod3, lat_row, norm_g[li, 1], w_out, wf_in, wf_out, fg,
                               kind=1, tm=tm_l, f_chunk=f_chunk, name=f"out_ffn_l{li}")
            if not last:
                flat = lambda a: a.reshape(1, b * lc, a.shape[-1])
                mix_c = (flat(outs[0][0]), flat(outs[1][0]), flat(outs[0][2]), flat(outs[1][2]),
                         flat(p_c), gla_norm_g[o])
                xc = _out_ffn_call(xc, mix_c, mod3, ctx_row, norm_g[li, 1], w_out, wf_in, wf_out, None,
                                   kind=1, tm=tm_c, f_chunk=f_chunk, name=f"out_ffn_c{li}")
    return xl
```

```python
import functools
import math

import jax
import jax.numpy as jnp
from jax import lax
from jax.experimental import pallas as pl
from jax.experimental.pallas import tpu as pltpu

F32 = jnp.float32
BF16 = jnp.bfloat16

NORM_EPS = 1e-6
ROPE_THETA = 10000.0
ROPE_DIM = 64
GRID_W = 64
MASK_VALUE = -1e30

DA_HEADS = 4
DA_QK_DIM = 64
DA_V_DIM = 128
WA_Q_HEADS = 8
WA_KV_HEADS = 2
WA_GROUP = WA_Q_HEADS // WA_KV_HEADS
WA_HEAD_DIM = 64
WINDOW = 128

GLA_HEADS = 4
GLA_K_DIM = 64
GLA_V_DIM = 128
GLA_GATE_RANK = 16
GLA_GATE_NORM = 16.0
LRU_WIDTH = 512
LRU_BLOCKS = 8
LRU_CONV = 4
LRU_C = 8.0

LANES = 128
SUBLANES = 8
VMEM_LIMIT_BYTES = 56 * 1024 * 1024

GLA_CHUNK = 64
GLA_SUB = 16
EVEN_IN = 2304
ODD_IN_PAD = 2688


def _params(*sem):
    return pltpu.CompilerParams(dimension_semantics=sem, vmem_limit_bytes=VMEM_LIMIT_BYTES)


def _resident(block_shape, index_map):
    return pl.BlockSpec(block_shape, index_map, pipeline_mode=pl.Buffered(1))


def _softplus(x):
    return jnp.maximum(x, 0.0) + jnp.log1p(jnp.exp(-jnp.abs(x)))


def _silu(x):
    return x * jax.nn.sigmoid(x)


def _gelu_tanh(x):
    return 0.5 * x * (1.0 + jnp.tanh(math.sqrt(2.0 / math.pi) * (x + 0.044715 * (x * x * x))))


def _dot_nt(a, b):
    return lax.dot_general(a, b, (((1,), (1,)), ((), ())), preferred_element_type=F32)


def _dot_tn(a, b):
    return lax.dot_general(a, b, (((0,), (0,)), ((), ())), preferred_element_type=F32)


def _ada_kernel(cond_ref, w_ref, b_ref, o_ref):
    s = _silu(cond_ref[...])
    o_ref[...] = jnp.dot(s, w_ref[...], preferred_element_type=F32,
                         precision=lax.Precision.HIGHEST) + b_ref[...]


def _ada_call(cond, ada_w, ada_b):
    depth, d, n = ada_w.shape
    tn = 1536
    return pl.pallas_call(
        _ada_kernel,
        grid=(depth, n // tn),
        in_specs=[pl.BlockSpec((SUBLANES, d), lambda l, j: (0, 0)),
                  pl.BlockSpec((None, d, tn), lambda l, j: (l, 0, j)),
                  pl.BlockSpec((None, 1, tn), lambda l, j: (l, 0, j))],
        out_specs=pl.BlockSpec((None, SUBLANES, tn), lambda l, j: (l, 0, j)),
        out_shape=jax.ShapeDtypeStruct((depth, SUBLANES, n), F32),
        compiler_params=_params("arbitrary", "arbitrary"),
        name="ada_mod",
    )(cond, ada_w, ada_b.reshape(depth, 1, n))


def _norm_mod(x, g, shift, scale):
    ms = jnp.mean(x * x, axis=-1, keepdims=True)
    return (x * lax.rsqrt(ms + NORM_EPS) * g) * (1.0 + scale) + shift


def _in_proj_kernel(*refs, rope_groups, n_chunk):
    if rope_groups:
        x_ref, mod_ref, g_ref, w_ref, cs_ref, o_ref = refs
    else:
        x_ref, mod_ref, g_ref, w_ref, o_ref = refs
    x = x_ref[...]
    d = x.shape[-1]
    mod = mod_ref[...]
    hb = _norm_mod(x, g_ref[...], mod[:, 0:d], mod[:, d:2 * d]).astype(BF16)
    n = w_ref.shape[1]
    if rope_groups:
        cos = cs_ref[:, 0:LANES]
        sin = cs_ref[:, LANES:2 * LANES]
        lane = lax.broadcasted_iota(jnp.int32, (x.shape[0], LANES), 1)
        first_half = (lane % 32) < 16
    for n0 in range(0, n, n_chunk):
        nw = min(n_chunk, n - n0)
        acc = jnp.dot(hb, w_ref[:, n0:n0 + nw], preferred_element_type=F32)
        if not rope_groups:
            o_ref[:, n0:n0 + nw] = acc.astype(o_ref.dtype)
            continue
        for j in range(nw // LANES):
            grp = acc[:, j * LANES:(j + 1) * LANES]
            if (n0 // LANES + j) in rope_groups:
                partner = jnp.where(first_half, pltpu.roll(grp, LANES - 16, 1), pltpu.roll(grp, 16, 1))
                grp = grp * cos + partner * sin
            c0 = n0 + j * LANES
            o_ref[:, c0:c0 + LANES] = grp.astype(o_ref.dtype)


def _in_proj_call(x, mod3, mod_row, g, w, cs, *, tm, n_chunk, rope_groups, out_dtype, name):
    b, t, d = x.shape
    n = w.shape[1]
    in_specs = [pl.BlockSpec((None, tm, d), lambda bi, i: (bi, i, 0)),
                pl.BlockSpec((None, 1, mod3.shape[-1]), lambda bi, i: (mod_row(bi), 0, 0)),
                pl.BlockSpec((1, d), lambda bi, i: (0, 0)),
                _resident((d, n), lambda bi, i: (0, 0))]
    args = [x, mod3, g.reshape(1, d), w]
    if rope_groups:
        in_specs.append(pl.BlockSpec((tm, 2 * LANES), lambda bi, i: (i, 0)))
        args.append(cs)
    return pl.pallas_call(
        functools.partial(_in_proj_kernel, rope_groups=rope_groups, n_chunk=n_chunk),
        grid=(b, t // tm),
        in_specs=in_specs,
        out_specs=pl.BlockSpec((None, tm, n), lambda bi, i: (bi, i, 0)),
        out_shape=jax.ShapeDtypeStruct((b, t, n), out_dtype),
        compiler_params=_params("parallel", "parallel"),
        name=name,
    )(*args)


def _diff_attn_kernel(*refs, tk, has_lat, lam_init, n_row_blocks):
    if has_lat:
        lam_ref, q_ref, kc_ref, vc_ref, kl_ref, vl_ref, o_ref, kx, vx, s_a, s_b = refs
    else:
        lam_ref, q_ref, kc_ref, vc_ref, o_ref, kx, vx, s_a, s_b = refs
    lc = kc_ref.shape[0]
    t = kx.shape[0]

    @pl.when(pl.program_id(2) == 0)
    def _():
        kx[0:lc, :] = kc_ref[...]
        vx[0:lc, 0:DA_V_DIM] = vc_ref[...]
        if has_lat:
            kx[lc:t, :] = kl_ref[...]
            vx[lc:t, 0:DA_V_DIM] = vl_ref[...]
        lane_t = lax.broadcasted_iota(jnp.int32, (t, LANES), 1)
        vx[:, DA_V_DIM:DA_V_DIM + LANES] = jnp.where(lane_t == 0, 1.0, 0.0).astype(BF16)

    q = q_ref[...]
    tq = q.shape[0]
    lane = lax.broadcasted_iota(jnp.int32, q.shape, 1)
    zero = jnp.zeros_like(q)
    qmaps = (jnp.where(lane < DA_QK_DIM, q, zero), jnp.where(lane >= DA_QK_DIM, q, zero))
    rows = tq // n_row_blocks
    qm = [qmap[r * rows:(r + 1) * rows] for r in range(n_row_blocks) for qmap in qmaps]

    def chunk(j):
        return pl.ds(j * tk if isinstance(j, int) else pl.multiple_of(j * tk, tk), tk)

    def scores_into(buf, j):
        k = kx[chunk(j), :]
        for ci in range(len(qm)):
            buf[ci] = _dot_nt(qm[ci], k)

    def consume(buf, j, carry):
        v = vx[chunk(j), :]
        out = []
        for ci in range(len(qm)):
            m, acc = carry[ci]
            s = buf[ci]
            m_new = jnp.maximum(m, jnp.max(s, axis=-1, keepdims=True))
            p = jnp.exp2(s - m_new)
            acc = jnp.exp2(m - m_new) * acc + jnp.dot(p.astype(BF16), v, preferred_element_type=F32)
            out.append((m_new, acc))
        return tuple(out)

    n = t // tk
    n_pairs = (n - 1) // 2
    buf_a, buf_b = s_a, s_b
    scores_into(buf_a, 0)

    def body(i, carry):
        scores_into(buf_b, 2 * i + 1)
        carry = consume(buf_a, 2 * i, carry)
        scores_into(buf_a, 2 * i + 2)
        return consume(buf_b, 2 * i + 1, carry)

    carry = tuple((jnp.full((rows, 1), MASK_VALUE, F32), jnp.zeros((rows, DA_V_DIM + LANES), F32)) for _ in qm)
    for i in range(n_pairs):
        carry = body(i, carry)
    if n % 2 == 0:
        scores_into(buf_b, n - 1)
        carry = consume(buf_a, n - 2, carry)
        carry = consume(buf_b, n - 1, carry)
    else:
        carry = consume(buf_a, n - 1, carry)
    o = [acc[:, 0:DA_V_DIM] / acc[:, DA_V_DIM:DA_V_DIM + 1] for _, acc in carry]
    lv = lam_ref[...]
    lam = (jnp.exp(jnp.sum(lv[0:1] * lv[1:2], axis=-1, keepdims=True))
           - jnp.exp(jnp.sum(lv[2:3] * lv[3:4], axis=-1, keepdims=True)) + lam_init)
    for r in range(n_row_blocks):
        w = o[2 * r] - lam * o[2 * r + 1]
        y = w * lax.rsqrt(jnp.mean(w * w, axis=-1, keepdims=True) + NORM_EPS) * (1.0 - lam_init)
        o_ref[r * rows:(r + 1) * rows, :] = y.astype(o_ref.dtype)


def _diff_attn_call(lam_vec, pq, pc, pl_kv, *, tq, n_row_blocks, tk_cap, lam_init, name):
    b, t, _ = pq.shape
    lc = pc.shape[1]
    kcol, vcol = DA_HEADS, 2 * DA_HEADS
    in_specs = [pl.BlockSpec((4, DA_QK_DIM), lambda bi, h, i: (0, 0)),
                pl.BlockSpec((None, tq, LANES), lambda bi, h, i: (bi, i, h)),
                pl.BlockSpec((None, lc, LANES), lambda bi, h, i: (bi, 0, kcol + h)),
                pl.BlockSpec((None, lc, LANES), lambda bi, h, i: (bi, 0, vcol + h))]
    args = [lam_vec, pq, pc, pc]
    n_keys = lc
    if pl_kv is not None:
        l = pl_kv.shape[1]
        n_keys += l
        in_specs += [pl.BlockSpec((None, l, LANES), lambda bi, h, i: (bi, 0, kcol + h)),
                     pl.BlockSpec((None, l, LANES), lambda bi, h, i: (bi, 0, vcol + h))]
        args += [pl_kv, pl_kv]
    tk = max(c for c in range(LANES, tk_cap + 1, LANES) if n_keys % c == 0)
    return pl.pallas_call(
        functools.partial(_diff_attn_kernel, tk=tk, has_lat=pl_kv is not None, lam_init=lam_init,
                          n_row_blocks=n_row_blocks),
        grid=(b, DA_HEADS, t // tq),
        in_specs=in_specs,
        out_specs=pl.BlockSpec((None, tq, LANES), lambda bi, h, i: (bi, i, h)),
        out_shape=jax.ShapeDtypeStruct((b, t, DA_HEADS * DA_V_DIM), BF16),
        scratch_shapes=[pltpu.VMEM((n_keys, LANES), BF16), pltpu.VMEM((n_keys, DA_V_DIM + LANES), BF16),
                        pltpu.VMEM((2 * n_row_blocks, tq // n_row_blocks, tk), F32),
                        pltpu.VMEM((2 * n_row_blocks, tq // n_row_blocks, tk), F32)],
        compiler_params=_params("arbitrary", "arbitrary", "arbitrary"),
        name=name,
    )(*args)


def _win_attn_kernel(*refs, tq, span, has_win):
    if has_win:
        sink_ref, q_ref, kc_ref, vc_ref, kl_ref, vl_ref, o_ref = refs
    else:
        sink_ref, q_ref, kc_ref, vc_ref, o_ref = refs
    q = q_ref[...]
    kc = kc_ref[...]
    vc = vc_ref[...]
    if has_win:
        l = kl_ref.shape[0]
        q0 = pl.program_id(1) * tq
        start = pl.multiple_of(jnp.clip(q0 - WINDOW, 0, l - span), WINDOW)
        kw = kl_ref[pl.ds(start, span), :]
        vw = vl_ref[pl.ds(start, span), :]
        qpos = q0 + lax.broadcasted_iota(jnp.int32, (tq, span), 0)
        kpos = start + lax.broadcasted_iota(jnp.int32, (tq, span), 1)
        valid = jnp.abs(kpos - qpos) <= WINDOW
    for hk in range(WA_KV_HEADS):
        ks = slice(hk * WA_HEAD_DIM, (hk + 1) * WA_HEAD_DIM)
        kc_h, vc_h = kc[:, ks], vc[:, ks]
        if has_win:
            kw_h, vw_h = kw[:, ks], vw[:, ks]
        for g in range(WA_GROUP):
            h = hk * WA_GROUP + g
            hs = slice(h * WA_HEAD_DIM, (h + 1) * WA_HEAD_DIM)
            q_h = q[:, hs]
            sink = sink_ref[h] * math.log2(math.e)
            s_c = _dot_nt(q_h, kc_h)
            m = jnp.maximum(jnp.max(s_c, axis=-1, keepdims=True), sink)
            if has_win:
                s_w = jnp.where(valid, _dot_nt(q_h, kw_h), MASK_VALUE)
                m = jnp.maximum(m, jnp.max(s_w, axis=-1, keepdims=True))
            p_c = jnp.exp2(s_c - m)
            den = jnp.sum(p_c, axis=-1, keepdims=True) + jnp.exp2(sink - m)
            num = jnp.dot(p_c.astype(BF16), vc_h, preferred_element_type=F32)
            if has_win:
                p_w = jnp.exp2(s_w - m)
                den = den + jnp.sum(p_w, axis=-1, keepdims=True)
                num = num + jnp.dot(p_w.astype(BF16), vw_h, preferred_element_type=F32)
            o_ref[:, hs] = (num / den).astype(o_ref.dtype)


def _win_attn_call(sink, pq, pc, pl_kv, *, tq, name):
    b, t, _ = pq.shape
    lc = pc.shape[1]
    qcol = 3
    kcol, vcol = 16, 17
    in_specs = [pl.BlockSpec(memory_space=pltpu.SMEM),
                pl.BlockSpec((None, tq, 4 * LANES), lambda bi, i: (bi, i, qcol)),
                pl.BlockSpec((None, lc, LANES), lambda bi, i: (bi, 0, kcol)),
                pl.BlockSpec((None, lc, LANES), lambda bi, i: (bi, 0, vcol))]
    args = [sink, pq, pc, pc]
    has_win = pl_kv is not None
    if has_win:
        l = pl_kv.shape[1]
        in_specs += [pl.BlockSpec((None, l, LANES), lambda bi, i: (bi, 0, kcol)),
                     pl.BlockSpec((None, l, LANES), lambda bi, i: (bi, 0, vcol))]
        args += [pl_kv, pl_kv]
    return pl.pallas_call(
        functools.partial(_win_attn_kernel, tq=tq, span=tq + 2 * WINDOW, has_win=has_win),
        grid=(b, t // tq),
        in_specs=in_specs,
        out_specs=pl.BlockSpec((None, tq, 4 * LANES), lambda bi, i: (bi, i, 0)),
        out_shape=jax.ShapeDtypeStruct((b, t, WA_Q_HEADS * WA_HEAD_DIM), BF16),
        compiler_params=_params("parallel", "arbitrary"),
        name=name,
    )(*args)


def _gla_chunk(q, k, v, g, state, rev):
    c, w = q.shape
    nsub = c // GLA_SUB
    row = lax.broadcasted_iota(jnp.int32, (c, w), 0)
    rb = row % GLA_SUB
    bl = g
    s = 1
    while s < GLA_SUB:
        if rev:
            bl = bl + jnp.where(rb < GLA_SUB - s, pltpu.roll(bl, c - s, 0), 0.0)
        else:
            bl = bl + jnp.where(rb >= s, pltpu.roll(bl, s, 0), 0.0)
        s *= 2
    order = list(range(nsub))[::-1] if rev else list(range(nsub))
    rank = {blk: n for n, blk in enumerate(order)}

    def rows(a, i):
        return a[i * GLA_SUB:(i + 1) * GLA_SUB]

    tot = {}
    for i in range(nsub):
        r = i * GLA_SUB if rev else i * GLA_SUB + GLA_SUB - 1
        tot[i] = bl[r:r + 1]
    pre = {}
    run = jnp.zeros((1, w), F32)
    for i in order:
        pre[i] = run
        run = run + tot[i]
    total = run

    def cat(fn):
        return jnp.concatenate([fn(i) for i in range(nsub)], axis=0)

    zeros = jnp.zeros((GLA_SUB, w), F32)
    b_full = cat(lambda i: rows(bl, i) + pre[i])
    qe = q * jnp.exp(b_full)
    kdec = k * jnp.exp(total - b_full)
    qd = q * jnp.exp(bl)
    kd = k * jnp.exp(-bl)
    ko = cat(lambda i: rows(k, i) * jnp.exp(tot[i] - rows(bl, i)))

    def stack_keys(x):
        r = lax.broadcasted_iota(jnp.int32, (GLA_HEADS * c, w), 0) // c
        ln = lax.broadcasted_iota(jnp.int32, (GLA_HEADS * c, w), 1) // GLA_K_DIM
        return jnp.where(r == ln, jnp.concatenate([x] * GLA_HEADS, axis=0), 0.0).astype(BF16)

    qi = lax.broadcasted_iota(jnp.int32, (c, GLA_HEADS * c), 0)
    kj = lax.broadcasted_iota(jnp.int32, (c, GLA_HEADS * c), 1) % c
    same = (qi // GLA_SUB) == (kj // GLA_SUB)
    tri = (qi <= kj) if rev else (qi >= kj)
    a = jnp.where(same & tri, _dot_nt(qd.astype(BF16), stack_keys(kd)), 0.0)
    for j in order[:-1]:
        off = pre[j] + tot[j]
        qo = cat(lambda i: rows(q, i) * jnp.exp(rows(b_full, i) - off) if rank[i] > rank[j] else zeros)
        koj = cat(lambda i: rows(ko, i) if i == j else zeros)
        a = a + _dot_nt(qo.astype(BF16), stack_keys(koj))

    vr = lax.broadcasted_iota(jnp.int32, (GLA_HEADS * c, v.shape[1]), 0) // c
    vl = lax.broadcasted_iota(jnp.int32, (GLA_HEADS * c, v.shape[1]), 1) // GLA_V_DIM
    vst = jnp.where(vr == vl, jnp.concatenate([v] * GLA_HEADS, axis=0), 0.0).astype(BF16)
    o = _dot_nt(qe.astype(BF16), state.astype(BF16)) + jnp.dot(a.astype(BF16), vst, preferred_element_type=F32)

    sr = lax.broadcasted_iota(jnp.int32, state.shape, 0) // GLA_V_DIM
    sl = lax.broadcasted_iota(jnp.int32, state.shape, 1) // GLA_K_DIM
    upd = _dot_tn(v.astype(BF16), kdec.astype(BF16))
    new_state = state * jnp.exp(total) + jnp.where(sr == sl, upd, 0.0)
    return o, new_state


def _gla_kernel(q_ref, k_ref, v_ref, lr_ref, gw_ref, gb_ref, s0_ref, o_ref, sT_ref, s_scr, *, rev, nc):
    i = pl.program_id(1)
    nbatch = q_ref.shape[0]

    @pl.when(i == 0)
    def _():
        s_scr[...] = s0_ref[...]

    gates, states = [], []
    for bb in range(nbatch):
        logit = jnp.dot(lr_ref[bb], gw_ref[...], preferred_element_type=F32) + gb_ref[...]
        gates.append((jnp.minimum(logit, 0.0) - jnp.log1p(jnp.exp(-jnp.abs(logit)))) * (1.0 / GLA_GATE_NORM))
        states.append(s_scr[bb])
    for n in range(nc):
        c0 = ((nc - 1 - n) if rev else n) * GLA_CHUNK
        sl = slice(c0, c0 + GLA_CHUNK)
        for bb in range(nbatch):
            o, states[bb] = _gla_chunk(q_ref[bb, sl, :], k_ref[bb, sl, :], v_ref[bb, sl, :], gates[bb][sl],
                                       states[bb], rev)
            o_ref[bb, sl, :] = o
    for bb in range(nbatch):
        s_scr[bb] = states[bb]

    @pl.when(i == pl.num_programs(1) - 1)
    def _():
        sT_ref[...] = s_scr[...]


def _gla_call(p, gw, gb, s0, *, tb, rev, name):
    b, t, _ = p.shape
    nb = t // tb
    hk = GLA_HEADS * GLA_K_DIM
    hv = GLA_HEADS * GLA_V_DIM
    bb = 2 if b % 2 == 0 else 1

    def tblk(i):
        return (nb - 1 - i) if rev else i

    return pl.pallas_call(
        functools.partial(_gla_kernel, rev=rev, nc=tb // GLA_CHUNK),
        grid=(b // bb, nb),
        in_specs=[pl.BlockSpec((bb, tb, hk), lambda bi, i: (bi, tblk(i), 0)),
                  pl.BlockSpec((bb, tb, hk), lambda bi, i: (bi, tblk(i), 1)),
                  pl.BlockSpec((bb, tb, hv), lambda bi, i: (bi, tblk(i), 1)),
                  pl.BlockSpec((bb, tb, LANES), lambda bi, i: (bi, tblk(i), 2560 // LANES)),
                  pl.BlockSpec((LANES, hk), lambda bi, i: (0, 0)),
                  pl.BlockSpec((1, hk), lambda bi, i: (0, 0)),
                  pl.BlockSpec((bb, hv, hk), lambda bi, i: (bi, 0, 0))],
        out_specs=[pl.BlockSpec((bb, tb, hv), lambda bi, i: (bi, tblk(i), 0)),
                   pl.BlockSpec((bb, hv, hk), lambda bi, i: (bi, 0, 0))],
        out_shape=[jax.ShapeDtypeStruct((b, t, hv), F32), jax.ShapeDtypeStruct((b, hv, hk), F32)],
        scratch_shapes=[pltpu.VMEM((bb, hv, hk), F32)],
        compiler_params=_params("parallel", "arbitrary"),
        name=name,
    )(p, p, p, p, gw, gb, s0)


def _lru_kernel(zx_ref, zp_ref, zn_ref, cw_ref, cb_ref, wg_ref, bg_ref, lam_ref, h0_ref,
                h_ref, hT_ref, carry_scr, *, rev, nb):
    i = pl.program_id(1)
    blk = (nb - 1 - i) if rev else i

    @pl.when(i == 0)
    def _():
        carry_scr[...] = h0_ref[...]

    x = zx_ref[...]
    tb, w = x.shape
    row = lax.broadcasted_iota(jnp.int32, (tb, w), 0)
    pm = jnp.where(blk > 0, zp_ref[SUBLANES - 1:SUBLANES, :], 0.0)
    n0 = jnp.where(blk < nb - 1, zn_ref[0:1, :], 0.0)
    n1 = jnp.where(blk < nb - 1, zn_ref[1:2, :], 0.0)
    xm1 = jnp.where(row == 0, pm, pltpu.roll(x, 1, 0))
    xp1 = jnp.where(row == tb - 1, n0, pltpu.roll(x, tb - 1, 0))
    xp2 = jnp.where(row == tb - 1, n1, jnp.where(row == tb - 2, n0, pltpu.roll(x, tb - 2, 0)))
    cw = cw_ref[...]
    xr = xm1 * cw[0:1] + x * cw[1:2] + xp1 * cw[2:3] + xp2 * cw[3:4] + cb_ref[...]

    gm = jnp.dot(xr.astype(BF16), wg_ref[...], preferred_element_type=F32) + bg_ref[...]
    r = 0.5 * jnp.tanh(0.5 * gm[:, 0:w]) + 0.5
    ig = 0.5 * jnp.tanh(0.5 * gm[:, w:2 * w]) + 0.5
    log_a = (-LRU_C * r) * _softplus(-lam_ref[...])
    a = jnp.exp(log_a)
    u = jnp.sqrt(-jnp.tanh(log_a) * (a * a + 1.0)) * (ig * xr)

    sub = lax.broadcasted_iota(jnp.int32, (SUBLANES, w), 0)
    h_prev = carry_scr[0:1, :]
    ng = tb // SUBLANES
    for gi in (range(ng - 1, -1, -1) if rev else range(ng)):
        rs = slice(gi * SUBLANES, (gi + 1) * SUBLANES)
        ag, ug = a[rs], u[rs]
        s = 1
        while s < SUBLANES:
            if rev:
                keep = sub < SUBLANES - s
                a_sh = jnp.where(keep, pltpu.roll(ag, SUBLANES - s, 0), 1.0)
                u_sh = jnp.where(keep, pltpu.roll(ug, SUBLANES - s, 0), 0.0)
            else:
                keep = sub >= s
                a_sh = jnp.where(keep, pltpu.roll(ag, s, 0), 1.0)
                u_sh = jnp.where(keep, pltpu.roll(ug, s, 0), 0.0)
            ug = ag * u_sh + ug
            ag = ag * a_sh
            s *= 2
        hg = ug + ag * h_prev
        h_ref[rs, :] = hg
        h_prev = hg[0:1] if rev else hg[SUBLANES - 1:SUBLANES]
    carry_scr[...] = jnp.broadcast_to(h_prev, carry_scr.shape)

    @pl.when(i == nb - 1)
    def _():
        hT_ref[...] = carry_scr[...]


def _lru_call(p, cw, cb, wg, bg, lam, h0, *, tb, rev, name):
    b, t, _ = p.shape
    nb = t // tb
    w = LRU_WIDTH
    zcol = 2048 // w
    per = tb // SUBLANES

    def tblk(i):
        return (nb - 1 - i) if rev else i

    return pl.pallas_call(
        functools.partial(_lru_kernel, rev=rev, nb=nb),
        grid=(b, nb),
        in_specs=[pl.BlockSpec((None, tb, w), lambda bi, i: (bi, tblk(i), zcol)),
                  pl.BlockSpec((None, SUBLANES, w), lambda bi, i: (bi, jnp.maximum(tblk(i) * per - 1, 0), zcol)),
                  pl.BlockSpec((None, SUBLANES, w),
                               lambda bi, i: (bi, jnp.minimum((tblk(i) + 1) * per, nb * per - 1), zcol)),
                  pl.BlockSpec((LRU_CONV, w), lambda bi, i: (0, 0)),
                  pl.BlockSpec((1, w), lambda bi, i: (0, 0)),
                  pl.BlockSpec((w, 2 * w), lambda bi, i: (0, 0)),
                  pl.BlockSpec((1, 2 * w), lambda bi, i: (0, 0)),
                  pl.BlockSpec((1, w), lambda bi, i: (0, 0)),
                  pl.BlockSpec((None, SUBLANES, w), lambda bi, i: (bi, 0, 0))],
        out_specs=[pl.BlockSpec((None, tb, w), lambda bi, i: (bi, tblk(i), 0)),
                   pl.BlockSpec((None, SUBLANES, w), lambda bi, i: (bi, 0, 0))],
        out_shape=[jax.ShapeDtypeStruct((b, t, w), F32), jax.ShapeDtypeStruct((b, SUBLANES, w), F32)],
        scratch_shapes=[pltpu.VMEM((SUBLANES, w), F32)],
        compiler_params=_params("parallel", "arbitrary"),
        name=name,
    )(p, p, p, cw, cb, wg, bg, lam, h0)


def _out_ffn_kernel(*refs, kind, final, f_chunk):
    if kind == 0:
        x_ref, ma_ref, mb_ref = refs[0:3]
        rest = refs[3:]
    else:
        x_ref, of_ref, ob_ref, hf_ref, hb_ref, gg_ref, zg_ref, glag_ref = refs[0:8]
        rest = refs[8:]
    if final:
        mod_ref, g2_ref, wo_ref, wi_ref, wf_ref, fg_ref, o_ref = rest
    else:
        mod_ref, g2_ref, wo_ref, wi_ref, wf_ref, o_ref = rest
    x = x_ref[...]
    d = x.shape[-1]
    if kind == 0:
        mix_a = ma_ref[...]
        mix_b = mb_ref[...]
    else:
        og = of_ref[...] + ob_ref[...]
        gg = gg_ref[...]
        parts = []
        for h in range(GLA_HEADS):
            hs = slice(h * GLA_V_DIM, (h + 1) * GLA_V_DIM)
            seg = og[:, hs]
            nrm = seg * lax.rsqrt(jnp.mean(seg * seg, axis=-1, keepdims=True) + NORM_EPS) * glag_ref[...]
            parts.append(nrm * _silu(gg[:, hs]))
        mix_a = jnp.concatenate(parts, axis=-1).astype(BF16)
        mix_b = ((hf_ref[...] + hb_ref[...]) * _gelu_tanh(zg_ref[...])).astype(BF16)
    wa = mix_a.shape[-1]
    y = (jnp.dot(mix_a, wo_ref[0:wa, :], preferred_element_type=F32)
         + jnp.dot(mix_b, wo_ref[wa:, :], preferred_element_type=F32))
    mod = mod_ref[...]
    x1 = x + mod[:, 2 * d:3 * d] * y
    h2 = _norm_mod(x1, g2_ref[...], mod[:, 3 * d:4 * d], mod[:, 4 * d:5 * d]).astype(BF16)
    f = wf_ref.shape[0]
    acc = None
    for f0 in range(0, f, f_chunk):
        gate = jnp.dot(h2, wi_ref[:, f0:f0 + f_chunk], preferred_element_type=F32)
        up = jnp.dot(h2, wi_ref[:, f + f0:f + f0 + f_chunk], preferred_element_type=F32)
        act = (_silu(gate) * up).astype(BF16)
        part = jnp.dot(act, wf_ref[f0:f0 + f_chunk, :], preferred_element_type=F32)
        acc = part if acc is None else acc + part
    x2 = x1 + mod[:, 5 * d:6 * d] * acc
    if final:
        ms = jnp.mean(x2 * x2, axis=-1, keepdims=True)
        x2 = x2 * lax.rsqrt(ms + NORM_EPS) * fg_ref[...]
    o_ref[...] = x2


def _out_ffn_call(x, mix, mod3, mod_row, g2, wo, wi, wf, final_g, *, kind, tm, f_chunk, name):
    b, t, d = x.shape
    f = wf.shape[0]

    def tile(width, col=0):
        return pl.BlockSpec((None, tm, width), lambda bi, i: (bi, i, col))

    in_specs = [tile(d)]
    args = [x]
    if kind == 0:
        ma, mb = mix
        in_specs += [tile(ma.shape[-1]), tile(mb.shape[-1])]
        args += [ma, mb]
    else:
        o_f, o_b, h_f, h_b, p, gla_g = mix
        hv = GLA_HEADS * GLA_V_DIM
        in_specs += [tile(hv), tile(hv), tile(LRU_WIDTH), tile(LRU_WIDTH),
                     tile(hv, 1024 // hv), tile(LRU_WIDTH, 1536 // LRU_WIDTH),
                     pl.BlockSpec((1, GLA_V_DIM), lambda bi, i: (0, 0))]
        args += [o_f, o_b, h_f, h_b, p, p, gla_g.reshape(1, GLA_V_DIM)]
    in_specs += [pl.BlockSpec((None, 1, mod3.shape[-1]), lambda bi, i: (mod_row(bi), 0, 0)),
                 pl.BlockSpec((1, d), lambda bi, i: (0, 0)),
                 _resident(wo.shape, lambda bi, i: (0, 0)),
                 _resident(wi.shape, lambda bi, i: (0, 0)),
                 _resident(wf.shape, lambda bi, i: (0, 0))]
    args += [mod3, g2.reshape(1, d), wo, wi, wf]
    final = final_g is not None
    if final:
        in_specs.append(pl.BlockSpec((1, d), lambda bi, i: (0, 0)))
        args.append(final_g.reshape(1, d))
    return pl.pallas_call(
        functools.partial(_out_ffn_kernel, kind=kind, final=final, f_chunk=f_chunk),
        grid=(b, t // tm),
        in_specs=in_specs,
        out_specs=pl.BlockSpec((None, tm, d), lambda bi, i: (bi, i, 0)),
        out_shape=jax.ShapeDtypeStruct((b, t, d), F32),
        compiler_params=_params("parallel", "parallel"),
        name=name,
    )(*args)


def _rope_table(n_tokens):
    n_rows = n_tokens // GRID_W
    row = jnp.broadcast_to(jnp.arange(n_rows)[:, None], (n_rows, GRID_W)).reshape(-1)
    col = jnp.broadcast_to(jnp.arange(GRID_W)[None, :], (n_rows, GRID_W)).reshape(-1)
    axis_dim = ROPE_DIM // 2
    inv_freq = ROPE_THETA ** (-jnp.arange(0, axis_dim, 2, dtype=F32) / axis_dim)
    ang_r = row.astype(F32)[:, None] * inv_freq
    ang_c = col.astype(F32)[:, None] * inv_freq
    cr, sr, cc, sc = jnp.cos(ang_r), jnp.sin(ang_r), jnp.cos(ang_c), jnp.sin(ang_c)
    cos = jnp.concatenate([cr, cr, cc, cc], axis=-1)
    sin = jnp.concatenate([-sr, sr, -sc, sc], axis=-1)
    reps = LANES // ROPE_DIM
    return jnp.concatenate([jnp.tile(cos, (1, reps)), jnp.tile(sin, (1, reps))], axis=-1)


def _even_w_in(w):
    qa, ka, va, qb, kb, vb = jnp.split(w, (512, 1024, 1536, 2048, 2176), axis=-1)
    qa_scale = DA_QK_DIM ** -0.5 * math.log2(math.e)
    qb_scale = WA_HEAD_DIM ** -0.5 * math.log2(math.e)
    return jnp.concatenate([qa * qa_scale, ka, va, qb * qb_scale, kb, vb], axis=-1).astype(BF16)


def _odd_w_in(w):
    q, k, v, g, lr, zg, zx = jnp.split(w, (256, 512, 1024, 1536, 1568, 2080), axis=-1)
    pad = jnp.zeros((w.shape[0], LANES - lr.shape[1]), w.dtype)
    return jnp.concatenate([q * GLA_K_DIM ** -0.5, k, v, g, zg, zx, lr, pad], axis=-1).astype(BF16)


def _block_diag(w):
    n, c, d = w.shape
    eye = jnp.eye(n, dtype=w.dtype)
    return (eye[:, None, :, None] * w[:, :, None, :]).reshape(n * c, n * d)


def _tile_rows(n, cap):
    t = min(n, cap)
    while n % t:
        t //= 2
    return t


def kernel(x, c, ctx, c_ctx, ada_w, ada_b, norm_g, even_w_in, even_w_out, diff_lam, win_sink, odd_w_in, odd_w_out, gla_gate_w, gla_gate_b, gla_norm_g, lru_conv_w, lru_conv_b, lru_wa, lru_ba, lru_wx, lru_bx, lru_lam, ffn_w_in, ffn_w_out, final_g):
    b, l, d = x.shape
    lc = ctx.shape[1]
    depth = ada_w.shape[0]
    assert b < SUBLANES and d == 1024

    cond = jnp.zeros((SUBLANES, d), F32).at[0:b].set(c).at[b].set(c_ctx)
    mod = _ada_call(cond, ada_w, ada_b)
    cs = _rope_table(l)

    xl = x
    xc = ctx.reshape(1, b * lc, d)
    tm_l = _tile_rows(l, 512)
    tm_c = _tile_rows(b * lc, 512)
    lat_row = lambda bi: bi
    ctx_row = lambda bi: b

    for li in range(depth):
        last = li == depth - 1
        mod3 = mod[li].reshape(SUBLANES, 1, 6 * d)
        wf_in = ffn_w_in[li].astype(BF16)
        wf_out = ffn_w_out[li].astype(BF16)
        fg = final_g if last else None
        f_chunk = wf_out.shape[0] // 2
        if li % 2 == 0:
            e = li // 2
            lam_init = 0.8 - 0.6 * math.exp(-0.3 * li)
            w_in = _even_w_in(even_w_in[e])
            w_out = even_w_out[e].astype(BF16)
            rope_groups = frozenset(range(0, 8)) | frozenset(range(12, 17))
            p_l = _in_proj_call(xl, mod3, lat_row, norm_g[li, 0], w_in, cs, tm=tm_l, n_chunk=768,
                                rope_groups=rope_groups, out_dtype=BF16, name=f"in_proj_l{li}")
            p_c = _in_proj_call(xc, mod3, ctx_row, norm_g[li, 0], w_in, None, tm=tm_c, n_chunk=768,
                                rope_groups=None, out_dtype=BF16, name=f"in_proj_c{li}")
            p_c = p_c.reshape(b, lc, EVEN_IN)
            tq = _tile_rows(l, 256)
            a_l = _diff_attn_call(diff_lam[e], p_l, p_c, p_l, tq=_tile_rows(l, 512), n_row_blocks=4, tk_cap=2816,
                                  lam_init=lam_init, name=f"diff_attn_l{li}")
            b_l = _win_attn_call(win_sink[e], p_l, p_c, p_l, tq=tq, name=f"win_attn_l{li}")
            xl = _out_ffn_call(xl, (a_l, b_l), mod3, lat_row, norm_g[li, 1], w_out, wf_in, wf_out, fg,
                               kind=0, tm=tm_l, f_chunk=f_chunk, name=f"out_ffn_l{li}")
            if not last:
                a_c = _diff_attn_call(diff_lam[e], p_c, p_c, None, tq=lc, n_row_blocks=2, tk_cap=768,
                                      lam_init=lam_init, name=f"diff_attn_c{li}")
                b_c = _win_attn_call(win_sink[e], p_c, p_c, None, tq=lc, name=f"win_attn_c{li}")
                mix_c = (a_c.reshape(1, b * lc, -1), b_c.reshape(1, b * lc, -1))
                xc = _out_ffn_call(xc, mix_c, mod3, ctx_row, norm_g[li, 1], w_out, wf_in, wf_out, None,
                                   kind=0, tm=tm_c, f_chunk=f_chunk, name=f"out_ffn_c{li}")
        else:
            o = li // 2
            w_in = _odd_w_in(odd_w_in[o])
            w_out = odd_w_out[o].astype(BF16)
            p_l = _in_proj_call(xl, mod3, lat_row, norm_g[li, 0], w_in, None, tm=tm_l, n_chunk=768,
                                rope_groups=None, out_dtype=F32, name=f"in_proj_l{li}")
            p_c = _in_proj_call(xc, mod3, ctx_row, norm_g[li, 0], w_in, None, tm=tm_c, n_chunk=768,
                                rope_groups=None, out_dtype=F32, name=f"in_proj_c{li}")
            p_c = p_c.reshape(b, lc, ODD_IN_PAD)
            hk = GLA_HEADS * GLA_K_DIM
            hv = GLA_HEADS * GLA_V_DIM
            s0 = jnp.zeros((b, hv, hk), F32)
            h0 = jnp.zeros((b, SUBLANES, LRU_WIDTH), F32)
            cw = lru_conv_w[o]
            cb = lru_conv_b[o].reshape(1, LRU_WIDTH)
            outs = []
            for dr in range(2):
                rev = dr == 1
                gw = jnp.zeros((LANES, hk), F32).at[dr * GLA_GATE_RANK:(dr + 1) * GLA_GATE_RANK].set(gla_gate_w[o, dr])
                gb = gla_gate_b[o, dr].reshape(1, hk)
                o_c, s_c = _gla_call(p_c, gw, gb, s0, tb=_tile_rows(lc, 256), rev=rev, name=f"gla_c{li}d{dr}")
                o_l, _ = _gla_call(p_l, gw, gb, s_c, tb=_tile_rows(l, 512), rev=rev, name=f"gla_l{li}d{dr}")
                wg = jnp.concatenate([_block_diag(lru_wa[o, dr]), _block_diag(lru_wx[o, dr])], axis=-1).astype(BF16)
                bg = jnp.concatenate([lru_ba[o, dr], lru_bx[o, dr]]).reshape(1, 2 * LRU_WIDTH)
                lam = lru_lam[o, dr].reshape(1, LRU_WIDTH)
                h_c, hT = _lru_call(p_c, cw, cb, wg, bg, lam, h0, tb=_tile_rows(lc, 256), rev=rev,
                                    name=f"lru_c{li}d{dr}")
                h_l, _ = _lru_call(p_l, cw, cb, wg, bg, lam, hT, tb=_tile_rows(l, 512), rev=rev,
                                   name=f"lru_l{li}d{dr}")
                outs.append((o_c, o_l, h_c, h_l))
            mix_l = (outs[0][1], outs[1][1], outs[0][3], outs[1][3], p_l, gla_norm_g[o])
            xl = _out_ffn_call(xl, mix_l, mod3, lat_row, norm_g[li, 1], w_out, wf_in, wf_out, fg,
                               kind=1, tm=tm_l, f_chunk=f_chunk, name=f"out_ffn_l{li}")
            if not last:
                flat = lambda a: a.reshape(1, b * lc, a.shape[-1])
                mix_c = (flat(outs[0][0]), flat(outs[1][0]), flat(outs[0][2]), flat(outs[1][2]),
                         flat(p_c), gla_norm_g[o])
                xc = _out_ffn_call(xc, mix_c, mod3, ctx_row, norm_g[li, 1], w_out, wf_in, wf_out, None,
                                   kind=1, tm=tm_c, f_chunk=f_chunk, name=f"out_ffn_c{li}")
    return xl
```

```python
import functools
import math

import jax
import jax.numpy as jnp
from jax import lax
from jax.experimental import pallas as pl
from jax.experimental.pallas import tpu as pltpu

F32 = jnp.float32
BF16 = jnp.bfloat16

NORM_EPS = 1e-6
ROPE_THETA = 10000.0
ROPE_DIM = 64
GRID_W = 64
MASK_VALUE = -1e30

DA_HEADS = 4
DA_QK_DIM = 64
DA_V_DIM = 128
WA_Q_HEADS = 8
WA_KV_HEADS = 2
WA_GROUP = WA_Q_HEADS // WA_KV_HEADS
WA_HEAD_DIM = 64
WINDOW = 128

GLA_HEADS = 4
GLA_K_DIM = 64
GLA_V_DIM = 128
GLA_GATE_RANK = 16
GLA_GATE_NORM = 16.0
LRU_WIDTH = 512
LRU_BLOCKS = 8
LRU_CONV = 4
LRU_C = 8.0

LANES = 128
SUBLANES = 8
VMEM_LIMIT_BYTES = 56 * 1024 * 1024

GLA_CHUNK = 64
GLA_SUB = 16
EVEN_IN = 2304
ODD_IN_PAD = 2688


def _params(*sem):
    return pltpu.CompilerParams(dimension_semantics=sem, vmem_limit_bytes=VMEM_LIMIT_BYTES)


def _resident(block_shape, index_map):
    return pl.BlockSpec(block_shape, index_map, pipeline_mode=pl.Buffered(1))


def _softplus(x):
    return jnp.maximum(x, 0.0) + jnp.log1p(jnp.exp(-jnp.abs(x)))


def _silu(x):
    return x * jax.nn.sigmoid(x)


def _gelu_tanh(x):
    return 0.5 * x * (1.0 + jnp.tanh(math.sqrt(2.0 / math.pi) * (x + 0.044715 * (x * x * x))))


def _dot_nt(a, b):
    return lax.dot_general(a, b, (((1,), (1,)), ((), ())), preferred_element_type=F32)


def _dot_tn(a, b):
    return lax.dot_general(a, b, (((0,), (0,)), ((), ())), preferred_element_type=F32)


def _ada_kernel(cond_ref, w_ref, b_ref, o_ref):
    s = _silu(cond_ref[...])
    o_ref[...] = jnp.dot(s, w_ref[...], preferred_element_type=F32,
                         precision=lax.Precision.HIGHEST) + b_ref[...]


def _ada_call(cond, ada_w, ada_b):
    depth, d, n = ada_w.shape
    tn = 1536
    return pl.pallas_call(
        _ada_kernel,
        grid=(depth, n // tn),
        in_specs=[pl.BlockSpec((SUBLANES, d), lambda l, j: (0, 0)),
                  pl.BlockSpec((None, d, tn), lambda l, j: (l, 0, j)),
                  pl.BlockSpec((None, 1, tn), lambda l, j: (l, 0, j))],
        out_specs=pl.BlockSpec((None, SUBLANES, tn), lambda l, j: (l, 0, j)),
        out_shape=jax.ShapeDtypeStruct((depth, SUBLANES, n), F32),
        compiler_params=_params("arbitrary", "arbitrary"),
        name="ada_mod",
    )(cond, ada_w, ada_b.reshape(depth, 1, n))


def _norm_mod(x, g, shift, scale):
    ms = jnp.mean(x * x, axis=-1, keepdims=True)
    return (x * lax.rsqrt(ms + NORM_EPS) * g) * (1.0 + scale) + shift


def _in_proj_kernel(*refs, rope_groups, n_chunk):
    if rope_groups:
        x_ref, mod_ref, g_ref, w_ref, cs_ref, o_ref = refs
    else:
        x_ref, mod_ref, g_ref, w_ref, o_ref = refs
    x = x_ref[...]
    d = x.shape[-1]
    mod = mod_ref[...]
    hb = _norm_mod(x, g_ref[...], mod[:, 0:d], mod[:, d:2 * d]).astype(BF16)
    n = w_ref.shape[1]
    if rope_groups:
        cos = cs_ref[:, 0:LANES]
        sin = cs_ref[:, LANES:2 * LANES]
        lane = lax.broadcasted_iota(jnp.int32, (x.shape[0], LANES), 1)
        first_half = (lane % 32) < 16
    for n0 in range(0, n, n_chunk):
        nw = min(n_chunk, n - n0)
        acc = jnp.dot(hb, w_ref[:, n0:n0 + nw], preferred_element_type=F32)
        if not rope_groups:
            o_ref[:, n0:n0 + nw] = acc.astype(o_ref.dtype)
            continue
        for j in range(nw // LANES):
            grp = acc[:, j * LANES:(j + 1) * LANES]
            if (n0 // LANES + j) in rope_groups:
                partner = jnp.where(first_half, pltpu.roll(grp, LANES - 16, 1), pltpu.roll(grp, 16, 1))
                grp = grp * cos + partner * sin
            c0 = n0 + j * LANES
            o_ref[:, c0:c0 + LANES] = grp.astype(o_ref.dtype)


def _in_proj_call(x, mod3, mod_row, g, w, cs, *, tm, n_chunk, rope_groups, out_dtype, name):
    b, t, d = x.shape
    n = w.shape[1]
    in_specs = [pl.BlockSpec((None, tm, d), lambda bi, i: (bi, i, 0)),
                pl.BlockSpec((None, 1, mod3.shape[-1]), lambda bi, i: (mod_row(bi), 0, 0)),
                pl.BlockSpec((1, d), lambda bi, i: (0, 0)),
                _resident((d, n), lambda bi, i: (0, 0))]
    args = [x, mod3, g.reshape(1, d), w]
    if rope_groups:
        in_specs.append(pl.BlockSpec((tm, 2 * LANES), lambda bi, i: (i, 0)))
        args.append(cs)
    return pl.pallas_call(
        functools.partial(_in_proj_kernel, rope_groups=rope_groups, n_chunk=n_chunk),
        grid=(b, t // tm),
        in_specs=in_specs,
        out_specs=pl.BlockSpec((None, tm, n), lambda bi, i: (bi, i, 0)),
        out_shape=jax.ShapeDtypeStruct((b, t, n), out_dtype),
        compiler_params=_params("parallel", "parallel"),
        name=name,
    )(*args)


def _diff_attn_kernel(*refs, tk, has_lat, lam_init, n_row_blocks):
    if has_lat:
        lam_ref, q_ref, kc_ref, vc_ref, kl_ref, vl_ref, o_ref, kx, vx, s_a, s_b = refs
    else:
        lam_ref, q_ref, kc_ref, vc_ref, o_ref, kx, vx, s_a, s_b = refs
    lc = kc_ref.shape[0]
    t = kx.shape[0]

    @pl.when(pl.program_id(2) == 0)
    def _():
        kx[0:lc, :] = kc_ref[...]
        vx[0:lc, 0:DA_V_DIM] = vc_ref[...]
        if has_lat:
            kx[lc:t, :] = kl_ref[...]
            vx[lc:t, 0:DA_V_DIM] = vl_ref[...]
        lane_t = lax.broadcasted_iota(jnp.int32, (t, LANES), 1)
        vx[:, DA_V_DIM:DA_V_DIM + LANES] = jnp.where(lane_t == 0, 1.0, 0.0).astype(BF16)

    q = q_ref[...]
    tq = q.shape[0]
    lane = lax.broadcasted_iota(jnp.int32, q.shape, 1)
    zero = jnp.zeros_like(q)
    qmaps = (jnp.where(lane < DA_QK_DIM, q, zero), jnp.where(lane >= DA_QK_DIM, q, zero))
    rows = tq // n_row_blocks
    qm = [qmap[r * rows:(r + 1) * rows] for r in range(n_row_blocks) for qmap in qmaps]

    def chunk(j):
        return pl.ds(j * tk if isinstance(j, int) else pl.multiple_of(j * tk, tk), tk)

    def scores_into(buf, j):
        k = kx[chunk(j), :]
        for ci in range(len(qm)):
            buf[ci] = _dot_nt(qm[ci], k)

    def consume(buf, j, carry):
        v = vx[chunk(j), :]
        out = []
        for ci in range(len(qm)):
            m, acc = carry[ci]
            s = buf[ci]
            m_new = jnp.maximum(m, jnp.max(s, axis=-1, keepdims=True))
            p = jnp.exp2(s - m_new)
            acc = jnp.exp2(m - m_new) * acc + jnp.dot(p.astype(BF16), v, preferred_element_type=F32)
            out.append((m_new, acc))
        return tuple(out)

    n = t // tk
    n_pairs = (n - 1) // 2
    buf_a, buf_b = s_a, s_b
    scores_into(buf_a, 0)

    def body(i, carry):
        scores_into(buf_b, 2 * i + 1)
        carry = consume(buf_a, 2 * i, carry)
        scores_into(buf_a, 2 * i + 2)
        return consume(buf_b, 2 * i + 1, carry)

    carry = tuple((jnp.full((rows, 1), MASK_VALUE, F32), jnp.zeros((rows, DA_V_DIM + LANES), F32)) for _ in qm)
    for i in range(n_pairs):
        carry = body(i, carry)
    if n % 2 == 0:
        scores_into(buf_b, n - 1)
        carry = consume(buf_a, n - 2, carry)
        carry = consume(buf_b, n - 1, carry)
    else:
        carry = consume(buf_a, n - 1, carry)
    o = [acc[:, 0:DA_V_DIM] / acc[:, DA_V_DIM:DA_V_DIM + 1] for _, acc in carry]
    lv = lam_ref[...]
    lam = (jnp.exp(jnp.sum(lv[0:1] * lv[1:2], axis=-1, keepdims=True))
           - jnp.exp(jnp.sum(lv[2:3] * lv[3:4], axis=-1, keepdims=True)) + lam_init)
    for r in range(n_row_blocks):
        w = o[2 * r] - lam * o[2 * r + 1]
        y = w * lax.rsqrt(jnp.mean(w * w, axis=-1, keepdims=True) + NORM_EPS) * (1.0 - lam_init)
        o_ref[r * rows:(r + 1) * rows, :] = y.astype(o_ref.dtype)


def _diff_attn_call(lam_vec, pq, pc, pl_kv, *, tq, n_row_blocks, tk_cap, lam_init, name):
    b, t, _ = pq.shape
    lc = pc.shape[1]
    kcol, vcol = DA_HEADS, 2 * DA_HEADS
    in_specs = [pl.BlockSpec((4, DA_QK_DIM), lambda bi, h, i: (0, 0)),
                pl.BlockSpec((None, tq, LANES), lambda bi, h, i: (bi, i, h)),
                pl.BlockSpec((None, lc, LANES), lambda bi, h, i: (bi, 0, kcol + h)),
                pl.BlockSpec((None, lc, LANES), lambda bi, h, i: (bi, 0, vcol + h))]
    args = [lam_vec, pq, pc, pc]
    n_keys = lc
    if pl_kv is not None:
        l = pl_kv.shape[1]
        n_keys += l
        in_specs += [pl.BlockSpec((None, l, LANES), lambda bi, h, i: (bi, 0, kcol + h)),
                     pl.BlockSpec((None, l, LANES), lambda bi, h, i: (bi, 0, vcol + h))]
        args += [pl_kv, pl_kv]
    tk = max(c for c in range(LANES, tk_cap + 1, LANES) if n_keys % c == 0)
    return pl.pallas_call(
        functools.partial(_diff_attn_kernel, tk=tk, has_lat=pl_kv is not None, lam_init=lam_init,
                          n_row_blocks=n_row_blocks),
        grid=(b, DA_HEADS, t // tq),
        in_specs=in_specs,
        out_specs=pl.BlockSpec((None, tq, LANES), lambda bi, h, i: (bi, i, h)),
        out_shape=jax.ShapeDtypeStruct((b, t, DA_HEADS * DA_V_DIM), BF16),
        scratch_shapes=[pltpu.VMEM((n_keys, LANES), BF16), pltpu.VMEM((n_keys, DA_V_DIM + LANES), BF16),
                        pltpu.VMEM((2 * n_row_blocks, tq // n_row_blocks, tk), F32),
                        pltpu.VMEM((2 * n_row_blocks, tq // n_row_blocks, tk), F32)],
        compiler_params=_params("arbitrary", "arbitrary", "arbitrary"),
        name=name,
    )(*args)


def _win_attn_kernel(*refs, tq, span, has_win):
    if has_win:
        sink_ref, q_ref, kc_ref, vc_ref, kl_ref, vl_ref, o_ref = refs
    else:
        sink_ref, q_ref, kc_ref, vc_ref, o_ref = refs
    q = q_ref[...]
    kc = kc_ref[...]
    vc = vc_ref[...]
    if has_win:
        l = kl_ref.shape[0]
        q0 = pl.program_id(1) * tq
        start = pl.multiple_of(jnp.clip(q0 - WINDOW, 0, l - span), WINDOW)
        kw = kl_ref[pl.ds(start, span), :]
        vw = vl_ref[pl.ds(start, span), :]
        qpos = q0 + lax.broadcasted_iota(jnp.int32, (tq, span), 0)
        kpos = start + lax.broadcasted_iota(jnp.int32, (tq, span), 1)
        valid = jnp.abs(kpos - qpos) <= WINDOW
    for hk in range(WA_KV_HEADS):
        ks = slice(hk * WA_HEAD_DIM, (hk + 1) * WA_HEAD_DIM)
        kc_h, vc_h = kc[:, ks], vc[:, ks]
        if has_win:
            kw_h, vw_h = kw[:, ks], vw[:, ks]
        for g in range(WA_GROUP):
            h = hk * WA_GROUP + g
            hs = slice(h * WA_HEAD_DIM, (h + 1) * WA_HEAD_DIM)
            q_h = q[:, hs]
            sink = sink_ref[h] * math.log2(math.e)
            s_c = _dot_nt(q_h, kc_h)
            m = jnp.maximum(jnp.max(s_c, axis=-1, keepdims=True), sink)
            if has_win:
                s_w = jnp.where(valid, _dot_nt(q_h, kw_h), MASK_VALUE)
                m = jnp.maximum(m, jnp.max(s_w, axis=-1, keepdims=True))
            p_c = jnp.exp2(s_c - m)
            den = jnp.sum(p_c, axis=-1, keepdims=True) + jnp.exp2(sink - m)
            num = jnp.dot(p_c.astype(BF16), vc_h, preferred_element_type=F32)
            if has_win:
                p_w = jnp.exp2(s_w - m)
                den = den + jnp.sum(p_w, axis=-1, keepdims=True)
                num = num + jnp.dot(p_w.astype(BF16), vw_h, preferred_element_type=F32)
            o_ref[:, hs] = (num / den).astype(o_ref.dtype)


def _win_attn_call(sink, pq, pc, pl_kv, *, tq, name):
    b, t, _ = pq.shape
    lc = pc.shape[1]
    qcol = 3
    kcol, vcol = 16, 17
    in_specs = [pl.BlockSpec(memory_space=pltpu.SMEM),
                pl.BlockSpec((None, tq, 4 * LANES), lambda bi, i: (bi, i, qcol)),
                pl.BlockSpec((None, lc, LANES), lambda bi, i: (bi, 0, kcol)),
                pl.BlockSpec((None, lc, LANES), lambda bi, i: (bi, 0, vcol))]
    args = [sink, pq, pc, pc]
    has_win = pl_kv is not None
    if has_win:
        l = pl_kv.shape[1]
        in_specs += [pl.BlockSpec((None, l, LANES), lambda bi, i: (bi, 0, kcol)),
                     pl.BlockSpec((None, l, LANES), lambda bi, i: (bi, 0, vcol))]
        args += [pl_kv, pl_kv]
    return pl.pallas_call(
        functools.partial(_win_attn_kernel, tq=tq, span=tq + 2 * WINDOW, has_win=has_win),
        grid=(b, t // tq),
        in_specs=in_specs,
        out_specs=pl.BlockSpec((None, tq, 4 * LANES), lambda bi, i: (bi, i, 0)),
        out_shape=jax.ShapeDtypeStruct((b, t, WA_Q_HEADS * WA_HEAD_DIM), BF16),
        compiler_params=_params("parallel", "arbitrary"),
        name=name,
    )(*args)


def _gla_chunk(q, k, v, g, state, rev):
    c, w = q.shape
    nsub = c // GLA_SUB
    row = lax.broadcasted_iota(jnp.int32, (c, w), 0)
    rb = row % GLA_SUB
    bl = g
    s = 1
    while s < GLA_SUB:
        if rev:
            bl = bl + jnp.where(rb < GLA_SUB - s, pltpu.roll(bl, c - s, 0), 0.0)
        else:
            bl = bl + jnp.where(rb >= s, pltpu.roll(bl, s, 0), 0.0)
        s *= 2
    order = list(range(nsub))[::-1] if rev else list(range(nsub))
    rank = {blk: n for n, blk in enumerate(order)}

    def rows(a, i):
        return a[i * GLA_SUB:(i + 1) * GLA_SUB]

    tot = {}
    for i in range(nsub):
        r = i * GLA_SUB if rev else i * GLA_SUB + GLA_SUB - 1
        tot[i] = bl[r:r + 1]
    pre = {}
    run = jnp.zeros((1, w), F32)
    for i in order:
        pre[i] = run
        run = run + tot[i]
    total = run

    def cat(fn):
        return jnp.concatenate([fn(i) for i in range(nsub)], axis=0)

    zeros = jnp.zeros((GLA_SUB, w), F32)
    b_full = cat(lambda i: rows(bl, i) + pre[i])
    qe = q * jnp.exp(b_full)
    kdec = k * jnp.exp(total - b_full)
    qd = q * jnp.exp(bl)
    kd = k * jnp.exp(-bl)
    ko = cat(lambda i: rows(k, i) * jnp.exp(tot[i] - rows(bl, i)))

    def stack_keys(x):
        r = lax.broadcasted_iota(jnp.int32, (GLA_HEADS * c, w), 0) // c
        ln = lax.broadcasted_iota(jnp.int32, (GLA_HEADS * c, w), 1) // GLA_K_DIM
        return jnp.where(r == ln, jnp.concatenate([x] * GLA_HEADS, axis=0), 0.0).astype(BF16)

    qi = lax.broadcasted_iota(jnp.int32, (c, GLA_HEADS * c), 0)
    kj = lax.broadcasted_iota(jnp.int32, (c, GLA_HEADS * c), 1) % c
    same = (qi // GLA_SUB) == (kj // GLA_SUB)
    tri = (qi <= kj) if rev else (qi >= kj)
    a = jnp.where(same & tri, _dot_nt(qd.astype(BF16), stack_keys(kd)), 0.0)
    for j in order[:-1]:
        off = pre[j] + tot[j]
        qo = cat(lambda i: rows(q, i) * jnp.exp(rows(b_full, i) - off) if rank[i] > rank[j] else zeros)
        koj = cat(lambda i: rows(ko, i) if i == j else zeros)
        a = a + _dot_nt(qo.astype(BF16), stack_keys(koj))

    vr = lax.broadcasted_iota(jnp.int32, (GLA_HEADS * c, v.shape[1]), 0) // c
    vl = lax.broadcasted_iota(jnp.int32, (GLA_HEADS * c, v.shape[1]), 1) // GLA_V_DIM
    vst = jnp.where(vr == vl, jnp.concatenate([v] * GLA_HEADS, axis=0), 0.0).astype(BF16)
    o = _dot_nt(qe.astype(BF16), state.astype(BF16)) + jnp.dot(a.astype(BF16), vst, preferred_element_type=F32)

    sr = lax.broadcasted_iota(jnp.int32, state.shape, 0) // GLA_V_DIM
    sl = lax.broadcasted_iota(jnp.int32, state.shape, 1) // GLA_K_DIM
    upd = _dot_tn(v.astype(BF16), kdec.astype(BF16))
    new_state = state * jnp.exp(total) + jnp.where(sr == sl, upd, 0.0)
    return o, new_state


def _gla_kernel(q_ref, k_ref, v_ref, lr_ref, gw_ref, gb_ref, s0_ref, o_ref, sT_ref, s_scr, *, rev, nc):
    i = pl.program_id(1)
    nbatch = q_ref.shape[0]

    @pl.when(i == 0)
    def _():
        s_scr[...] = s0_ref[...]

    gates, states = [], []
    for bb in range(nbatch):
        logit = jnp.dot(lr_ref[bb], gw_ref[...], preferred_element_type=F32) + gb_ref[...]
        gates.append((jnp.minimum(logit, 0.0) - jnp.log1p(jnp.exp(-jnp.abs(logit)))) * (1.0 / GLA_GATE_NORM))
        states.append(s_scr[bb])
    for n in range(nc):
        c0 = ((nc - 1 - n) if rev else n) * GLA_CHUNK
        sl = slice(c0, c0 + GLA_CHUNK)
        for bb in range(nbatch):
            o, states[bb] = _gla_chunk(q_ref[bb, sl, :], k_ref[bb, sl, :], v_ref[bb, sl, :], gates[bb][sl],
                                       states[bb], rev)
            o_ref[bb, sl, :] = o
    for bb in range(nbatch):
        s_scr[bb] = states[bb]

    @pl.when(i == pl.num_programs(1) - 1)
    def _():
        sT_ref[...] = s_scr[...]


def _gla_call(p, gw, gb, s0, *, tb, rev, name):
    b, t, _ = p.shape
    nb = t // tb
    hk = GLA_HEADS * GLA_K_DIM
    hv = GLA_HEADS * GLA_V_DIM
    bb = 2 if b % 2 == 0 else 1

    def tblk(i):
        return (nb - 1 - i) if rev else i

    return pl.pallas_call(
        functools.partial(_gla_kernel, rev=rev, nc=tb // GLA_CHUNK),
        grid=(b // bb, nb),
        in_specs=[pl.BlockSpec((bb, tb, hk), lambda bi, i: (bi, tblk(i), 0)),
                  pl.BlockSpec((bb, tb, hk), lambda bi, i: (bi, tblk(i), 1)),
                  pl.BlockSpec((bb, tb, hv), lambda bi, i: (bi, tblk(i), 1)),
                  pl.BlockSpec((bb, tb, LANES), lambda bi, i: (bi, tblk(i), 2560 // LANES)),
                  pl.BlockSpec((LANES, hk), lambda bi, i: (0, 0)),
                  pl.BlockSpec((1, hk), lambda bi, i: (0, 0)),
                  pl.BlockSpec((bb, hv, hk), lambda bi, i: (bi, 0, 0))],
        out_specs=[pl.BlockSpec((bb, tb, hv), lambda bi, i: (bi, tblk(i), 0)),
                   pl.BlockSpec((bb, hv, hk), lambda bi, i: (bi, 0, 0))],
        out_shape=[jax.ShapeDtypeStruct((b, t, hv), F32), jax.ShapeDtypeStruct((b, hv, hk), F32)],
        scratch_shapes=[pltpu.VMEM((bb, hv, hk), F32)],
        compiler_params=_params("parallel", "arbitrary"),
        name=name,
    )(p, p, p, p, gw, gb, s0)


def _lru_kernel(zx_ref, zp_ref, zn_ref, cw_ref, cb_ref, wg_ref, bg_ref, lam_ref, h0_ref,
                h_ref, hT_ref, carry_scr, *, rev, nb):
    i = pl.program_id(1)
    blk = (nb - 1 - i) if rev else i

    @pl.when(i == 0)
    def _():
        carry_scr[...] = h0_ref[...]

    x = zx_ref[...]
    tb, w = x.shape
    row = lax.broadcasted_iota(jnp.int32, (tb, w), 0)
    pm = jnp.where(blk > 0, zp_ref[SUBLANES - 1:SUBLANES, :], 0.0)
    n0 = jnp.where(blk < nb - 1, zn_ref[0:1, :], 0.0)
    n1 = jnp.where(blk < nb - 1, zn_ref[1:2, :], 0.0)
    xm1 = jnp.where(row == 0, pm, pltpu.roll(x, 1, 0))
    xp1 = jnp.where(row == tb - 1, n0, pltpu.roll(x, tb - 1, 0))
    xp2 = jnp.where(row == tb - 1, n1, jnp.where(row == tb - 2, n0, pltpu.roll(x, tb - 2, 0)))
    cw = cw_ref[...]
    xr = xm1 * cw[0:1] + x * cw[1:2] + xp1 * cw[2:3] + xp2 * cw[3:4] + cb_ref[...]

    gm = jnp.dot(xr.astype(BF16), wg_ref[...], preferred_element_type=F32) + bg_ref[...]
    r = 0.5 * jnp.tanh(0.5 * gm[:, 0:w]) + 0.5
    ig = 0.5 * jnp.tanh(0.5 * gm[:, w:2 * w]) + 0.5
    log_a = (-LRU_C * r) * _softplus(-lam_ref[...])
    a = jnp.exp(log_a)
    u = jnp.sqrt(-jnp.tanh(log_a) * (a * a + 1.0)) * (ig * xr)

    sub = lax.broadcasted_iota(jnp.int32, (SUBLANES, w), 0)
    h_prev = carry_scr[0:1, :]
    ng = tb // SUBLANES
    for gi in (range(ng - 1, -1, -1) if rev else range(ng)):
        rs = slice(gi * SUBLANES, (gi + 1) * SUBLANES)
        ag, ug = a[rs], u[rs]
        s = 1
        while s < SUBLANES:
            if rev:
                keep = sub < SUBLANES - s
                a_sh = jnp.where(keep, pltpu.roll(ag, SUBLANES - s, 0), 1.0)
                u_sh = jnp.where(keep, pltpu.roll(ug, SUBLANES - s, 0), 0.0)
            else:
                keep = sub >= s
                a_sh = jnp.where(keep, pltpu.roll(ag, s, 0), 1.0)
                u_sh = jnp.where(keep, pltpu.roll(ug, s, 0), 0.0)
            ug = ag * u_sh + ug
            ag = ag * a_sh
            s *= 2
        hg = ug + ag * h_prev
        h_ref[rs, :] = hg
        h_prev = hg[0:1] if rev else hg[SUBLANES - 1:SUBLANES]
    carry_scr[...] = jnp.broadcast_to(h_prev, carry_scr.shape)

    @pl.when(i == nb - 1)
    def _():
        hT_ref[...] = carry_scr[...]


def _lru_call(p, cw, cb, wg, bg, lam, h0, *, tb, rev, name):
    b, t, _ = p.shape
    nb = t // tb
    w = LRU_WIDTH
    zcol = 2048 // w
    per = tb // SUBLANES

    def tblk(i):
        return (nb - 1 - i) if rev else i

    return pl.pallas_call(
        functools.partial(_lru_kernel, rev=rev, nb=nb),
        grid=(b, nb),
        in_specs=[pl.BlockSpec((None, tb, w), lambda bi, i: (bi, tblk(i), zcol)),
                  pl.BlockSpec((None, SUBLANES, w), lambda bi, i: (bi, jnp.maximum(tblk(i) * per - 1, 0), zcol)),
                  pl.BlockSpec((None, SUBLANES, w),
                               lambda bi, i: (bi, jnp.minimum((tblk(i) + 1) * per, nb * per - 1), zcol)),
                  pl.BlockSpec((LRU_CONV, w), lambda bi, i: (0, 0)),
                  pl.BlockSpec((1, w), lambda bi, i: (0, 0)),
                  pl.BlockSpec((w, 2 * w), lambda bi, i: (0, 0)),
                  pl.BlockSpec((1, 2 * w), lambda bi, i: (0, 0)),
                  pl.BlockSpec((1, w), lambda bi, i: (0, 0)),
                  pl.BlockSpec((None, SUBLANES, w), lambda bi, i: (bi, 0, 0))],
        out_specs=[pl.BlockSpec((None, tb, w), lambda bi, i: (bi, tblk(i), 0)),
                   pl.BlockSpec((None, SUBLANES, w), lambda bi, i: (bi, 0, 0))],
        out_shape=[jax.ShapeDtypeStruct((b, t, w), F32), jax.ShapeDtypeStruct((b, SUBLANES, w), F32)],
        scratch_shapes=[pltpu.VMEM((SUBLANES, w), F32)],
        compiler_params=_params("parallel", "arbitrary"),
        name=name,
    )(p, p, p, cw, cb, wg, bg, lam, h0)


def _out_ffn_kernel(*refs, kind, final, f_chunk):
    if kind == 0:
        x_ref, ma_ref, mb_ref = refs[0:3]
        rest = refs[3:]
    else:
        x_ref, of_ref, ob_ref, hf_ref, hb_ref, gg_ref, zg_ref, glag_ref = refs[0:8]
        rest = refs[8:]
    if final:
        mod_ref, g2_ref, wo_ref, wi_ref, wf_ref, fg_ref, o_ref = rest
    else:
        mod_ref, g2_ref, wo_ref, wi_ref, wf_ref, o_ref = rest
    x = x_ref[...]
    d = x.shape[-1]
    if kind == 0:
        mix_a = ma_ref[...]
        mix_b = mb_ref[...]
    else:
        og = of_ref[...] + ob_ref[...]
        gg = gg_ref[...]
        parts = []
        for h in range(GLA_HEADS):
            hs = slice(h * GLA_V_DIM, (h + 1) * GLA_V_DIM)
            seg = og[:, hs]
            nrm = seg * lax.rsqrt(jnp.mean(seg * seg, axis=-1, keepdims=True) + NORM_EPS) * glag_ref[...]
            parts.append(nrm * _silu(gg[:, hs]))
        mix_a = jnp.concatenate(parts, axis=-1).astype(BF16)
        mix_b = ((hf_ref[...] + hb_ref[...]) * _gelu_tanh(zg_ref[...])).astype(BF16)
    wa = mix_a.shape[-1]
    y = (jnp.dot(mix_a, wo_ref[0:wa, :], preferred_element_type=F32)
         + jnp.dot(mix_b, wo_ref[wa:, :], preferred_element_type=F32))
    mod = mod_ref[...]
    x1 = x + mod[:, 2 * d:3 * d] * y
    h2 = _norm_mod(x1, g2_ref[...], mod[:, 3 * d:4 * d], mod[:, 4 * d:5 * d]).astype(BF16)
    f = wf_ref.shape[0]
    acc = None
    for f0 in range(0, f, f_chunk):
        gate = jnp.dot(h2, wi_ref[:, f0:f0 + f_chunk], preferred_element_type=F32)
        up = jnp.dot(h2, wi_ref[:, f + f0:f + f0 + f_chunk], preferred_element_type=F32)
        act = (_silu(gate) * up).astype(BF16)
        part = jnp.dot(act, wf_ref[f0:f0 + f_chunk, :], preferred_element_type=F32)
        acc = part if acc is None else acc + part
    x2 = x1 + mod[:, 5 * d:6 * d] * acc
    if final:
        ms = jnp.mean(x2 * x2, axis=-1, keepdims=True)
        x2 = x2 * lax.rsqrt(ms + NORM_EPS) * fg_ref[...]
    o_ref[...] = x2


def _out_ffn_call(x, mix, mod3, mod_row, g2, wo, wi, wf, final_g, *, kind, tm, f_chunk, name):
    b, t, d = x.shape
    f = wf.shape[0]

    def tile(width, col=0):
        return pl.BlockSpec((None, tm, width), lambda bi, i: (bi, i, col))

    in_specs = [tile(d)]
    args = [x]
    if kind == 0:
        ma, mb = mix
        in_specs += [tile(ma.shape[-1]), tile(mb.shape[-1])]
        args += [ma, mb]
    else:
        o_f, o_b, h_f, h_b, p, gla_g = mix
        hv = GLA_HEADS * GLA_V_DIM
        in_specs += [tile(hv), tile(hv), tile(LRU_WIDTH), tile(LRU_WIDTH),
                     tile(hv, 1024 // hv), tile(LRU_WIDTH, 1536 // LRU_WIDTH),
                     pl.BlockSpec((1, GLA_V_DIM), lambda bi, i: (0, 0))]
        args += [o_f, o_b, h_f, h_b, p, p, gla_g.reshape(1, GLA_V_DIM)]
    in_specs += [pl.BlockSpec((None, 1, mod3.shape[-1]), lambda bi, i: (mod_row(bi), 0, 0)),
                 pl.BlockSpec((1, d), lambda bi, i: (0, 0)),
                 _resident(wo.shape, lambda bi, i: (0, 0)),
                 _resident(wi.shape, lambda bi, i: (0, 0)),
                 _resident(wf.shape, lambda bi, i: (0, 0))]
    args += [mod3, g2.reshape(1, d), wo, wi, wf]
    final = final_g is not None
    if final:
        in_specs.append(pl.BlockSpec((1, d), lambda bi, i: (0, 0)))
        args.append(final_g.reshape(1, d))
    return pl.pallas_call(
        functools.partial(_out_ffn_kernel, kind=kind, final=final, f_chunk=f_chunk),
        grid=(b, t // tm),
        in_specs=in_specs,
        out_specs=pl.BlockSpec((None, tm, d), lambda bi, i: (bi, i, 0)),
        out_shape=jax.ShapeDtypeStruct((b, t, d), F32),
        compiler_params=_params("parallel", "parallel"),
        name=name,
    )(*args)


def _rope_table(n_tokens):
    n_rows = n_tokens // GRID_W
    row = jnp.broadcast_to(jnp.arange(n_rows)[:, None], (n_rows, GRID_W)).reshape(-1)
    col = jnp.broadcast_to(jnp.arange(GRID_W)[None, :], (n_rows, GRID_W)).reshape(-1)
    axis_dim = ROPE_DIM // 2
    inv_freq = ROPE_THETA ** (-jnp.arange(0, axis_dim, 2, dtype=F32) / axis_dim)
    ang_r = row.astype(F32)[:, None] * inv_freq
    ang_c = col.astype(F32)[:, None] * inv_freq
    cr, sr, cc, sc = jnp.cos(ang_r), jnp.sin(ang_r), jnp.cos(ang_c), jnp.sin(ang_c)
    cos = jnp.concatenate([cr, cr, cc, cc], axis=-1)
    sin = jnp.concatenate([-sr, sr, -sc, sc], axis=-1)
    reps = LANES // ROPE_DIM
    return jnp.concatenate([jnp.tile(cos, (1, reps)), jnp.tile(sin, (1, reps))], axis=-1)


def _even_w_in(w):
    qa, ka, va, qb, kb, vb = jnp.split(w, (512, 1024, 1536, 2048, 2176), axis=-1)
    qa_scale = DA_QK_DIM ** -0.5 * math.log2(math.e)
    qb_scale = WA_HEAD_DIM ** -0.5 * math.log2(math.e)
    return jnp.concatenate([qa * qa_scale, ka, va, qb * qb_scale, kb, vb], axis=-1).astype(BF16)


def _odd_w_in(w):
    q, k, v, g, lr, zg, zx = jnp.split(w, (256, 512, 1024, 1536, 1568, 2080), axis=-1)
    pad = jnp.zeros((w.shape[0], LANES - lr.shape[1]), w.dtype)
    return jnp.concatenate([q * GLA_K_DIM ** -0.5, k, v, g, zg, zx, lr, pad], axis=-1).astype(BF16)


def _block_diag(w):
    n, c, d = w.shape
    eye = jnp.eye(n, dtype=w.dtype)
    return (eye[:, None, :, None] * w[:, :, None, :]).reshape(n * c, n * d)


def _tile_rows(n, cap):
    t = min(n, cap)
    while n % t:
        t //= 2
    return t


def kernel(x, c, ctx, c_ctx, ada_w, ada_b, norm_g, even_w_in, even_w_out, diff_lam, win_sink, odd_w_in, odd_w_out, gla_gate_w, gla_gate_b, gla_norm_g, lru_conv_w, lru_conv_b, lru_wa, lru_ba, lru_wx, lru_bx, lru_lam, ffn_w_in, ffn_w_out, final_g):
    b, l, d = x.shape
    lc = ctx.shape[1]
    depth = ada_w.shape[0]
    assert b < SUBLANES and d == 1024

    cond = jnp.zeros((SUBLANES, d), F32).at[0:b].set(c).at[b].set(c_ctx)
    mod = _ada_call(cond, ada_w, ada_b)
    cs = _rope_table(l)

    xl = x
    xc = ctx.reshape(1, b * lc, d)
    tm_l = _tile_rows(l, 512)
    tm_c = _tile_rows(b * lc, 512)
    lat_row = lambda bi: bi
    ctx_row = lambda bi: b

    for li in range(depth):
        last = li == depth - 1
        mod3 = mod[li].reshape(SUBLANES, 1, 6 * d)
        wf_in = ffn_w_in[li].astype(BF16)
        wf_out = ffn_w_out[li].astype(BF16)
        fg = final_g if last else None
        f_chunk = wf_out.shape[0] // 2
        if li % 2 == 0:
            e = li // 2
            lam_init = 0.8 - 0.6 * math.exp(-0.3 * li)
            w_in = _even_w_in(even_w_in[e])
            w_out = even_w_out[e].astype(BF16)
            rope_groups = frozenset(range(0, 8)) | frozenset(range(12, 17))
            p_l = _in_proj_call(xl, mod3, lat_row, norm_g[li, 0], w_in, cs, tm=tm_l, n_chunk=768,
                                rope_groups=rope_groups, out_dtype=BF16, name=f"in_proj_l{li}")
            p_c = _in_proj_call(xc, mod3, ctx_row, norm_g[li, 0], w_in, None, tm=tm_c, n_chunk=768,
                                rope_groups=None, out_dtype=BF16, name=f"in_proj_c{li}")
            p_c = p_c.reshape(b, lc, EVEN_IN)
            tq = _tile_rows(l, 256)
            a_l = _diff_attn_call(diff_lam[e], p_l, p_c, p_l, tq=_tile_rows(l, 512), n_row_blocks=4, tk_cap=2816,
                                  lam_init=lam_init, name=f"diff_attn_l{li}")
            b_l = _win_attn_call(win_sink[e], p_l, p_c, p_l, tq=tq, name=f"win_attn_l{li}")
            xl = _out_ffn_call(xl, (a_l, b_l), mod3, lat_row, norm_g[li, 1], w_out, wf_in, wf_out, fg,
                               kind=0, tm=tm_l, f_chunk=f_chunk, name=f"out_ffn_l{li}")
            if not last:
                a_c = _diff_attn_call(diff_lam[e], p_c, p_c, None, tq=lc, n_row_blocks=2, tk_cap=768,
                                      lam_init=lam_init, name=f"diff_attn_c{li}")
                b_c = _win_attn_call(win_sink[e], p_c, p_c, None, tq=lc, name=f"win_attn_c{li}")
                mix_c = (a_c.reshape(1, b * lc, -1), b_c.reshape(1, b * lc, -1))
                xc = _out_ffn_call(xc, mix_c, mod3, ctx_row, norm_g[li, 1], w_out, wf_in, wf_out, None,
                                   kind=0, tm=tm_c, f_chunk=f_chunk, name=f"out_ffn_c{li}")
        else:
            o = li // 2
            w_in = _odd_w_in(odd_w_in[o])
            w_out = odd_w_out[o].astype(BF16)
            p_l = _in_proj_call(xl, mod3, lat_row, norm_g[li, 0], w_in, None, tm=tm_l, n_chunk=768,
                                rope_groups=None, out_dtype=F32, name=f"in_proj_l{li}")
            p_c = _in_proj_call(xc, mod3, ctx_row, norm_g[li, 0], w_in, None, tm=tm_c, n_chunk=768,
                                rope_groups=None, out_dtype=F32, name=f"in_proj_c{li}")
            p_c = p_c.reshape(b, lc, ODD_IN_PAD)
            hk = GLA_HEADS * GLA_K_DIM
            hv = GLA_HEADS * GLA_V_DIM
            s0 = jnp.zeros((b, hv, hk), F32)
            h0 = jnp.zeros((b, SUBLANES, LRU_WIDTH), F32)
            cw = lru_conv_w[o]
            cb = lru_conv_b[o].reshape(1, LRU_WIDTH)
            outs = []
            for dr in range(2):
                rev = dr == 1
                gw = jnp.zeros((LANES, hk), F32).at[dr * GLA_GATE_RANK:(dr + 1) * GLA_GATE_RANK].set(gla_gate_w[o, dr])
                gb = gla_gate_b[o, dr].reshape(1, hk)
                o_c, s_c = _gla_call(p_c, gw, gb, s0, tb=_tile_rows(lc, 256), rev=rev, name=f"gla_c{li}d{dr}")
                o_l, _ = _gla_call(p_l, gw, gb, s_c, tb=_tile_rows(l, 512), rev=rev, name=f"gla_l{li}d{dr}")
                wg = jnp.concatenate([_block_diag(lru_wa[o, dr]), _block_diag(lru_wx[o, dr])], axis=-1).astype(BF16)
                bg = jnp.concatenate([lru_ba[o, dr], lru_bx[o, dr]]).reshape(1, 2 * LRU_WIDTH)
                lam = lru_lam[o, dr].reshape(1, LRU_WIDTH)
                h_c, hT = _lru_call(p_c, cw, cb, wg, bg, lam, h0, tb=_tile_rows(lc, 256), rev=rev,
                                    name=f"lru_c{li}d{dr}")
                h_l, _ = _lru_call(p_l, cw, cb, wg, bg, lam, hT, tb=_tile_rows(l, 1024), rev=rev,
                                   name=f"lru_l{li}d{dr}")
                outs.append((o_c, o_l, h_c, h_l))
            mix_l = (outs[0][1], outs[1][1], outs[0][3], outs[1][3], p_l, gla_norm_g[o])
            xl = _out_ffn_call(xl, mix_l, mod3, lat_row, norm_g[li, 1], w_out, wf_in, wf_out, fg,
                               kind=1, tm=tm_l, f_chunk=f_chunk, name=f"out_ffn_l{li}")
            if not last:
                flat = lambda a: a.reshape(1, b * lc, a.shape[-1])
                mix_c = (flat(outs[0][0]), flat(outs[1][0]), flat(outs[0][2]), flat(outs[1][2]),
                         flat(p_c), gla_norm_g[o])
                xc = _out_ffn_call(xc, mix_c, mod3, ctx_row, norm_g[li, 1], w_out, wf_in, wf_out, None,
                                   kind=1, tm=tm_c, f_chunk=f_chunk, name=f"out_ffn_c{li}")
    return xl
```
